```python
import math
import jax, jax.numpy as jnp
from jax import lax
import numpy as np

D_MODEL = 1024
BATCH = 4
SEQ = 4096
DEPTH = 2

RET_HEADS = 8
RET_DK = 64
RET_DV = 128
RET_CHUNK = 128
DIFF_HEADS = 8
DIFF_DK = 64
DIFF_DV = 2 * DIFF_DK
Q_BLOCK = 128
EPS = 1e-6

RET_QK = RET_HEADS * RET_DK
RET_V = RET_HEADS * RET_DV
DIFF_QK = DIFF_HEADS * 2 * DIFF_DK
DIFF_V = DIFF_HEADS * DIFF_DV
IN_SPLITS = (RET_QK, RET_QK, RET_V, RET_V, DIFF_QK, DIFF_QK, DIFF_V, DIFF_V, D_MODEL, D_MODEL)
N_IN = 2 * RET_QK + 2 * RET_V + 2 * DIFF_QK + 2 * DIFF_V + 2 * D_MODEL

kernel_name = "hybrid_retention_diffattn_gated_block"


def rms_norm(x, g):
    xf = x.astype(jnp.float32)
    xf = xf * lax.rsqrt(jnp.mean(xf * xf, axis=-1, keepdims=True) + EPS)
    return xf * g.astype(jnp.float32)


def retention(q, k, v):
    B, S, H, dk = q.shape
    dv = v.shape[-1]
    C = RET_CHUNK
    n = S // C
    log_gamma = jnp.log1p(-jnp.exp2(-5.0 - jnp.arange(H, dtype=jnp.float32)))
    pos = jnp.arange(C, dtype=jnp.float32)
    rel = pos[:, None] - pos[None, :]
    inner_decay = jnp.where(rel >= 0, jnp.exp(jnp.maximum(rel, 0.0)[None] * log_gamma[:, None, None]), 0.0)
    q_decay = jnp.exp((pos + 1.0)[None] * log_gamma[:, None])
    k_decay = jnp.exp((C - 1.0 - pos)[None] * log_gamma[:, None])
    chunk_decay = jnp.exp(C * log_gamma)

    def to_chunks(t):
        return t.astype(jnp.float32).reshape(B, n, C, H, t.shape[-1]).transpose(1, 0, 3, 2, 4)

    qc = to_chunks(q)
    kc = to_chunks(k) * (dk ** -0.5)
    vc = to_chunks(v)

    def step(state, inp):
        qi, ki, vi = inp
        scores = jnp.einsum('bhnd,bhmd->bhnm', qi, ki) * inner_decay
        out = (jnp.einsum('bhnm,bhme->bhne', scores, vi)
               + jnp.einsum('bhnd,bhde->bhne', qi, state) * q_decay[..., None])
        state = (state * chunk_decay[:, None, None]
                 + jnp.einsum('bhmd,bhme->bhde', ki * k_decay[..., None], vi))
        return state, out

    state0 = jnp.zeros((B, H, dk, dv), jnp.float32)
    _, out = lax.scan(step, state0, (qc, kc, vc))
    return out.transpose(1, 0, 3, 2, 4).reshape(B, S, H, dv)


def diff_attention(q, k, v, lam):
    B, S, H, _, dk = q.shape
    slopes = jnp.exp2(-8.0 * jnp.arange(1, H + 1, dtype=jnp.float32) / H)
    q = q.astype(jnp.float32) * (dk ** -0.5)
    k = k.astype(jnp.float32)
    v = v.astype(jnp.float32)
    outs = []
    for i in range(S // Q_BLOCK):
        start, end = i * Q_BLOCK, (i + 1) * Q_BLOCK
        qb = q[:, start:end]
        kb = k[:, :end]
        vb = v[:, :end]
        dist = (jnp.arange(start, end)[:, None] - jnp.arange(end)[None, :]).astype(jnp.float32)
        s = jnp.einsum('bqhjd,bkhjd->bhjqk', qb, kb) - slopes[None, :, None, None, None] * dist
        s = jnp.where(dist >= 0, s, -jnp.inf)
        p = jax.nn.softmax(s, axis=-1)
        a = p[:, :, 0] - lam * p[:, :, 1]
        outs.append(jnp.einsum('bhqk,bkhe->bqhe', a, vb))
    return jnp.concatenate(outs, axis=1)


def setup_inputs(seed: int = 0) -> dict:
    key = jax.random.key(seed)
    ks = jax.random.split(key, 15)
    f32 = jnp.float32
    n = lambda k, s: jax.random.normal(k, s, f32)
    return {
        "x": n(ks[0], (BATCH, SEQ, D_MODEL)),
        "norm_g": 1.0 + 0.02 * n(ks[1], (DEPTH, D_MODEL)),
        "w_in": n(ks[2], (DEPTH, D_MODEL, N_IN)) * D_MODEL ** -0.5,
        "ret_norm_g": 1.0 + 0.02 * n(ks[3], (DEPTH, RET_V)),
        "ret_w_o": n(ks[4], (DEPTH, RET_V, D_MODEL)) * RET_V ** -0.5,
        "diff_q_norm_g": 1.0 + 0.02 * n(ks[5], (DEPTH, DIFF_DK)),
        "diff_k_norm_g": 1.0 + 0.02 * n(ks[6], (DEPTH, DIFF_DK)),
        "diff_lq1": 0.1 * n(ks[7], (DEPTH, DIFF_DK)),
        "diff_lk1": 0.1 * n(ks[8], (DEPTH, DIFF_DK)),
        "diff_lq2": 0.1 * n(ks[9], (DEPTH, DIFF_DK)),
        "diff_lk2": 0.1 * n(ks[10], (DEPTH, DIFF_DK)),
        "diff_sub_norm_g": 1.0 + 0.02 * n(ks[11], (DEPTH, DIFF_V)),
        "diff_w_o": n(ks[12], (DEPTH, DIFF_V, D_MODEL)) * DIFF_V ** -0.5,
        "w_out": n(ks[13], (DEPTH, D_MODEL, D_MODEL)) * D_MODEL ** -0.5,
    }


def reference(x, norm_g, w_in, ret_norm_g, ret_w_o, diff_q_norm_g, diff_k_norm_g,
              diff_lq1, diff_lk1, diff_lq2, diff_lk2, diff_sub_norm_g, diff_w_o, w_out):
    B, S, _ = x.shape
    split_idx = [int(v) for v in np.cumsum(IN_SPLITS)[:-1]]
    for l in range(DEPTH):
        h = rms_norm(x, norm_g[l]).astype(x.dtype)
        z = jnp.einsum('bsd,dn->bsn', h, w_in[l])
        (q_r, k_r, v_r, gate_r, q_d, k_d, v_d, gate_d,
         mg_r, mg_d) = jnp.split(z, split_idx, axis=-1)

        o_r = retention(q_r.reshape(B, S, RET_HEADS, RET_DK),
                        k_r.reshape(B, S, RET_HEADS, RET_DK),
                        v_r.reshape(B, S, RET_HEADS, RET_DV))
        o_r = rms_norm(o_r, ret_norm_g[l].reshape(RET_HEADS, RET_DV)).reshape(B, S, RET_V)
        o_r = (o_r * jax.nn.silu(gate_r.astype(jnp.float32))).astype(x.dtype)
        y_r = jnp.einsum('bse,ed->bsd', o_r, ret_w_o[l])

        lam_init = 0.8 - 0.6 * math.exp(-0.3 * l)
        lam = (jnp.exp(jnp.sum(diff_lq1[l].astype(jnp.float32) * diff_lk1[l].astype(jnp.float32)))
               - jnp.exp(jnp.sum(diff_lq2[l].astype(jnp.float32) * diff_lk2[l].astype(jnp.float32)))
               + lam_init)
        qd = rms_norm(q_d.reshape(B, S, DIFF_HEADS, 2, DIFF_DK), diff_q_norm_g[l])
        kd = rms_norm(k_d.reshape(B, S, DIFF_HEADS, 2, DIFF_DK), diff_k_norm_g[l])
        o_d = diff_attention(qd, kd, v_d.reshape(B, S, DIFF_HEADS, DIFF_DV), lam)
        o_d = rms_norm(o_d, diff_sub_norm_g[l].reshape(DIFF_HEADS, DIFF_DV)) * (1.0 - lam_init)
        o_d = (o_d.reshape(B, S, DIFF_V) * jax.nn.silu(gate_d.astype(jnp.float32))).astype(x.dtype)
        y_d = jnp.einsum('bse,ed->bsd', o_d, diff_w_o[l])

        m = jax.nn.sigmoid(mg_r) * y_r + jax.nn.sigmoid(mg_d) * y_d
        x = (x + jnp.einsum('bsd,de->bse', m, w_out[l])).astype(x.dtype)
    return x
```

```python
import functools
import math

import jax
import jax.numpy as jnp
from jax import lax
from jax.experimental import pallas as pl
from jax.experimental.pallas import tpu as pltpu

D_MODEL = 1024
HEADS = 8
DK = 64
DV = 128
EPS = 1e-6
LANES = 128
N_IN = 9216

COL_QR, COL_KR, COL_VR, COL_GR = 0, 4, 8, 16
COL_QD, COL_KD, COL_VD, COL_GD = 24, 32, 40, 48
COL_MR, COL_MD = 56, 64

IN_TM, IN_TN = 1024, 1024
NORM_ROWS = 256
RET_CHUNK = 256
ATT_T = 256
OUT_TM = 512
NEG = -1e30

F32 = jnp.float32
BF16 = jnp.bfloat16
_NT = (((1,), (1,)), ((), ()))


def _inproj_kernel(x_ref, g_ref, w_ref, z_ref, h_ref):
    @pl.when(pl.program_id(1) == 0)
    def _():
        def body(r, carry):
            rows = pl.ds(pl.multiple_of(r * NORM_ROWS, NORM_ROWS), NORM_ROWS)
            x = x_ref[rows, :]
            ms = jnp.mean(x * x, axis=-1, keepdims=True)
            h_ref[rows, :] = (x * lax.rsqrt(ms + EPS) * g_ref[...]).astype(BF16)
            return carry
        lax.fori_loop(0, IN_TM // NORM_ROWS, body, 0)

    z_ref[...] = jnp.dot(h_ref[...], w_ref[...], preferred_element_type=F32).astype(z_ref.dtype)


def _inproj(x2, g, w):
    m = x2.shape[0]
    return pl.pallas_call(
        _inproj_kernel,
        grid=(m // IN_TM, N_IN // IN_TN),
        in_specs=[
            pl.BlockSpec((IN_TM, D_MODEL), lambda i, j: (i, 0)),
            pl.BlockSpec((1, D_MODEL), lambda i, j: (0, 0)),
            pl.BlockSpec((D_MODEL, IN_TN), lambda i, j: (0, j)),
        ],
        out_specs=pl.BlockSpec((IN_TM, IN_TN), lambda i, j: (i, j)),
        out_shape=jax.ShapeDtypeStruct((m, N_IN), BF16),
        scratch_shapes=[pltpu.VMEM((IN_TM, D_MODEL), BF16)],
        compiler_params=pltpu.CompilerParams(dimension_semantics=("parallel", "arbitrary")),
        name="inproj",
    )(x2, g, w)


def _retention_kernel(lg_ref, q_ref, k_ref, v_ref, gate_ref, g_ref, o_ref,
                      dmat_ref, kdec_ref, qdec_ref, sdec_ref, smask_ref, state_ref):
    c = RET_CHUNK
    seq = q_ref.shape[0]
    pair = pl.program_id(1)
    lga = lg_ref[2 * pair]
    lgb = lg_ref[2 * pair + 1]
    scale = DK ** -0.5

    n = lax.broadcasted_iota(jnp.int32, (c, c), 0)
    mm = lax.broadcasted_iota(jnp.int32, (c, c), 1)
    rel = (n - mm).astype(F32)
    causal = rel >= 0.0
    relc = jnp.maximum(rel, 0.0)
    dmat_ref[0:c, :] = jnp.where(causal, jnp.exp(relc * lga), 0.0) * scale
    dmat_ref[c:2 * c, :] = jnp.where(causal, jnp.exp(relc * lgb), 0.0) * scale
    pos_k = lax.broadcasted_iota(jnp.int32, (c, LANES), 0).astype(F32)
    lane_k = lax.broadcasted_iota(jnp.int32, (c, LANES), 1)
    kdec_ref[...] = jnp.exp((c - 1.0 - pos_k) * jnp.where(lane_k < DK, lga, lgb)) * scale
    pos_q = lax.broadcasted_iota(jnp.int32, (c, 2 * DV), 0).astype(F32)
    lane_q = lax.broadcasted_iota(jnp.int32, (c, 2 * DV), 1)
    qdec_ref[...] = jnp.exp((pos_q + 1.0) * jnp.where(lane_q < DV, lga, lgb))
    srow = lax.broadcasted_iota(jnp.int32, (2 * DK, 2 * DV), 0)
    scol = lax.broadcasted_iota(jnp.int32, (2 * DK, 2 * DV), 1)
    in_a = (srow < DK) & (scol < DV)
    in_b = (srow >= DK) & (scol >= DV)
    smask_ref[...] = jnp.where(in_a | in_b, 1.0, 0.0)
    sdec_ref[...] = (jnp.where(in_a, jnp.exp(jnp.full((2 * DK, 2 * DV), c, F32) * lga), 0.0)
                     + jnp.where(in_b, jnp.exp(jnp.full((2 * DK, 2 * DV), c, F32) * lgb), 0.0))
    state_ref[...] = jnp.zeros_like(state_ref)

    def chunk(ci, carry):
        rows = pl.ds(pl.multiple_of(ci * c, c), c)
        qc = q_ref[rows, :]
        kc = k_ref[rows, :]
        vc = v_ref[rows, :]
        q32 = qc.astype(F32)
        lane = lax.broadcasted_iota(jnp.int32, (c, LANES), 1)
        qa = jnp.where(lane < DK, q32, 0.0).astype(BF16)
        qb = jnp.where(lane >= DK, q32, 0.0).astype(BF16)
        qs = jnp.concatenate([qa, qb], axis=0)
        sc = lax.dot_general(qs, kc, _NT, preferred_element_type=F32)
        pm = (sc * dmat_ref[...]).astype(BF16)
        intra_a = jnp.dot(pm[0:c], vc[:, 0:DV], preferred_element_type=F32)
        intra_b = jnp.dot(pm[c:2 * c], vc[:, DV:2 * DV], preferred_element_type=F32)
        st = state_ref[...]
        inter = jnp.dot(qc, st.astype(BF16), preferred_element_type=F32)
        out = jnp.concatenate([intra_a, intra_b], axis=1) + inter * qdec_ref[...]

        kd_t = (kc.astype(F32) * kdec_ref[...]).T.astype(BF16)
        upd = jnp.dot(kd_t, vc, preferred_element_type=F32)
        state_ref[...] = st * sdec_ref[...] + upd * smask_ref[...]

        g = g_ref[...]
        gate = gate_ref[rows, :].astype(F32)
        act = gate * jax.nn.sigmoid(gate)
        halves = []
        for hh in range(2):
            o = out[:, hh * DV:(hh + 1) * DV]
            ms = jnp.mean(o * o, axis=-1, keepdims=True)
            halves.append(o * lax.rsqrt(ms + EPS) * g[:, hh * DV:(hh + 1) * DV])
        o_ref[rows, :] = (jnp.concatenate(halves, axis=1) * act).astype(o_ref.dtype)
        return carry

    lax.fori_loop(0, seq // c, chunk, 0)


def _retention(z3, log_gamma, g):
    b, seq, _ = z3.shape
    c = RET_CHUNK
    return pl.pallas_call(
        _retention_kernel,
        grid=(b, HEADS // 2),
        in_specs=[
            pl.BlockSpec(memory_space=pltpu.SMEM),
            pl.BlockSpec((None, seq, LANES), lambda i, p: (i, 0, COL_QR + p)),
            pl.BlockSpec((None, seq, LANES), lambda i, p: (i, 0, COL_KR + p)),
            pl.BlockSpec((None, seq, 2 * DV), lambda i, p: (i, 0, COL_VR // 2 + p)),
            pl.BlockSpec((None, seq, 2 * DV), lambda i, p: (i, 0, COL_GR // 2 + p)),
            pl.BlockSpec((1, 2 * DV), lambda i, p: (0, p)),
        ],
        out_specs=pl.BlockSpec((None, seq, 2 * DV), lambda i, p: (i, 0, p)),
        out_shape=jax.ShapeDtypeStruct((b, seq, HEADS * DV), BF16),
        scratch_shapes=[
            pltpu.VMEM((2 * c, c), F32),
            pltpu.VMEM((c, LANES), F32),
            pltpu.VMEM((c, 2 * DV), F32),
            pltpu.VMEM((2 * DK, 2 * DV), F32),
            pltpu.VMEM((2 * DK, 2 * DV), F32),
            pltpu.VMEM((2 * DK, 2 * DV), F32),
        ],
        compiler_params=pltpu.CompilerParams(dimension_semantics=("parallel", "parallel")),
        name="retention",
    )(log_gamma, z3, z3, z3, z3, g)


def _group_rms_norm(x, g):
    lane = lax.broadcasted_iota(jnp.int32, x.shape, 1)
    lo = lane < DK
    sq = x * x
    s_lo = jnp.sum(jnp.where(lo, sq, 0.0), axis=-1, keepdims=True)
    s_hi = jnp.sum(jnp.where(lo, 0.0, sq), axis=-1, keepdims=True)
    ms = jnp.where(lo, s_lo, s_hi) * (1.0 / DK)
    return x * lax.rsqrt(ms + EPS) * g


def _diffattn_kernel(slope_ref, q_ref, k_ref, v_ref, gate_ref, gq_ref, gk_ref,
                     lq1_ref, lk1_ref, lq2_ref, lk2_ref, gs_ref, o_ref,
                     kn_ref, qs_ref, m_ref, l_ref, acc_ref, *, lam_init):
    t = ATT_T
    seq = q_ref.shape[0]
    slope = slope_ref[pl.program_id(1)]
    lam = (jnp.exp(jnp.sum(lq1_ref[...] * lk1_ref[...], axis=-1, keepdims=True))
           - jnp.exp(jnp.sum(lq2_ref[...] * lk2_ref[...], axis=-1, keepdims=True))
           + lam_init)

    def knorm(r, carry):
        rows = pl.ds(pl.multiple_of(r * NORM_ROWS, NORM_ROWS), NORM_ROWS)
        kn_ref[rows, :] = _group_rms_norm(k_ref[rows, :].astype(F32), gk_ref[...]).astype(BF16)
        return carry
    lax.fori_loop(0, seq // NORM_ROWS, knorm, 0)

    mrow = lax.broadcasted_iota(jnp.int32, (2 * t, t), 0)
    mcol = lax.broadcasted_iota(jnp.int32, (2 * t, t), 1)
    diag_mask = mcol <= jnp.where(mrow >= t, mrow - t, mrow)
    col_pos = lax.broadcasted_iota(jnp.int32, (1, t), 1)

    def q_tile(qi, carry):
        rows = pl.ds(pl.multiple_of(qi * t, t), t)
        qn = _group_rms_norm(q_ref[rows, :].astype(F32), gq_ref[...]) * (DK ** -0.5)
        lane = lax.broadcasted_iota(jnp.int32, (t, LANES), 1)
        qs_ref[0:t, :] = jnp.where(lane < DK, qn, 0.0).astype(BF16)
        qs_ref[t:2 * t, :] = jnp.where(lane >= DK, qn, 0.0).astype(BF16)
        m_ref[...] = jnp.full_like(m_ref, NEG)
        l_ref[...] = jnp.zeros_like(l_ref)
        acc_ref[...] = jnp.zeros_like(acc_ref)

        def kv_tile(j, masked):
            cols = pl.ds(pl.multiple_of(j * t, t), t)
            s = lax.dot_general(qs_ref[...], kn_ref[cols, :], _NT, preferred_element_type=F32)
            s = s + slope * (col_pos + (j - qi) * t).astype(F32)
            if masked:
                s = jnp.where(diag_mask, s, NEG)
            m_old = m_ref[...]
            m_new = jnp.maximum(m_old, jnp.max(s, axis=-1, keepdims=True))
            alpha = jnp.exp(m_old - m_new)
            p = jnp.exp(s - m_new)
            l_ref[...] = alpha * l_ref[...] + jnp.sum(p, axis=-1, keepdims=True)
            acc_ref[...] = alpha * acc_ref[...] + jnp.dot(p.astype(BF16), v_ref[cols, :],
                                                          preferred_element_type=F32)
            m_ref[...] = m_new

        def off_diag(j, c2):
            kv_tile(j, False)
            return c2
        lax.fori_loop(0, qi, off_diag, 0)
        kv_tile(qi, True)

        inv = 1.0 / l_ref[...]
        o = acc_ref[0:t, :] * inv[0:t] - lam * (acc_ref[t:2 * t, :] * inv[t:2 * t])
        ms = jnp.mean(o * o, axis=-1, keepdims=True)
        on = o * lax.rsqrt(ms + EPS) * gs_ref[...] * (1.0 - lam_init)
        gate = gate_ref[rows, :].astype(F32)
        o_ref[rows, :] = (on * (gate * jax.nn.sigmoid(gate))).astype(o_ref.dtype)
        return carry

    lax.fori_loop(0, seq // t, q_tile, 0)


def _diffattn(z3, slopes, gq, gk, lq1, lk1, lq2, lk2, gs, lam_init):
    b, seq, _ = z3.shape
    t = ATT_T
    head_tile = lambda col: pl.BlockSpec((None, seq, LANES), lambda i, h: (i, 0, col + h))
    vec = lambda width: pl.BlockSpec((1, width), lambda i, h: (0, 0))
    return pl.pallas_call(
        functools.partial(_diffattn_kernel, lam_init=lam_init),
        grid=(b, HEADS),
        in_specs=[
            pl.BlockSpec(memory_space=pltpu.SMEM),
            head_tile(COL_QD), head_tile(COL_KD), head_tile(COL_VD), head_tile(COL_GD),
            vec(LANES), vec(LANES), vec(DK), vec(DK), vec(DK), vec(DK),
            pl.BlockSpec((1, DV), lambda i, h: (0, h)),
        ],
        out_specs=pl.BlockSpec((None, seq, DV), lambda i, h: (i, 0, h)),
        out_shape=jax.ShapeDtypeStruct((b, seq, HEADS * DV), BF16),
        scratch_shapes=[
            pltpu.VMEM((seq, LANES), BF16),
            pltpu.VMEM((2 * t, LANES), BF16),
            pltpu.VMEM((2 * t, 1), F32),
            pltpu.VMEM((2 * t, 1), F32),
            pltpu.VMEM((2 * t, DV), F32),
        ],
        compiler_params=pltpu.CompilerParams(dimension_semantics=("parallel", "parallel")),
        name="diffattn",
    )(slopes, z3, z3, z3, z3, gq, gk, lq1, lk1, lq2, lk2, gs)


def _outproj_kernel(x_ref, or_ref, od_ref, mr_ref, md_ref, wr_ref, wd_ref, wo_ref, out_ref):
    y_r = jnp.dot(or_ref[...], wr_ref[...], preferred_element_type=F32)
    y_d = jnp.dot(od_ref[...], wd_ref[...], preferred_element_type=F32)
    merged = (jax.nn.sigmoid(mr_ref[...].astype(F32)) * y_r
              + jax.nn.sigmoid(md_ref[...].astype(F32)) * y_d)
    out_ref[...] = x_ref[...] + jnp.dot(merged.astype(BF16), wo_ref[...], preferred_element_type=F32)


def _outproj(x2, o_r, o_d, z2, w_r, w_d, w_o):
    m = x2.shape[0]
    rows = lambda col: pl.BlockSpec((OUT_TM, D_MODEL), lambda i: (i, col))
    weight = pl.BlockSpec((D_MODEL, D_MODEL), lambda i: (0, 0))
    return pl.pallas_call(
        _outproj_kernel,
        grid=(m // OUT_TM,),
        in_specs=[rows(0), rows(0), rows(0),
                  rows(COL_MR * LANES // D_MODEL), rows(COL_MD * LANES // D_MODEL),
                  weight, weight, weight],
        out_specs=rows(0),
        out_shape=jax.ShapeDtypeStruct((m, D_MODEL), F32),
        compiler_params=pltpu.CompilerParams(dimension_semantics=("parallel",)),
        name="outproj",
    )(x2, o_r, o_d, z2, z2, w_r, w_d, w_o)


def kernel(x, norm_g, w_in, ret_norm_g, ret_w_o, diff_q_norm_g, diff_k_norm_g,
           diff_lq1, diff_lk1, diff_lq2, diff_lk2, diff_sub_norm_g, diff_w_o, w_out):
    b, seq, d = x.shape
    depth = norm_g.shape[0]
    assert d == D_MODEL and w_in.shape[-1] == N_IN
    assert seq % ATT_T == 0 and seq % RET_CHUNK == 0 and (b * seq) % IN_TM == 0

    heads = jnp.arange(HEADS, dtype=F32)
    log_gamma = jnp.log1p(-jnp.exp2(-5.0 - heads))
    slopes = jnp.exp2(-8.0 * (heads + 1.0) / HEADS)
    row = lambda v: v.reshape(1, -1).astype(F32)
    pair = lambda v: jnp.concatenate([v, v]).reshape(1, -1).astype(F32)

    x2 = x.reshape(b * seq, d)
    for l in range(depth):
        lam_init = 0.8 - 0.6 * math.exp(-0.3 * l)
        z2 = _inproj(x2, row(norm_g[l]), w_in[l].astype(BF16))
        z3 = z2.reshape(b, seq, N_IN)
        o_r = _retention(z3, log_gamma, row(ret_norm_g[l]))
        o_d = _diffattn(z3, slopes, pair(diff_q_norm_g[l]), pair(diff_k_norm_g[l]),
                        row(diff_lq1[l]), row(diff_lk1[l]), row(diff_lq2[l]), row(diff_lk2[l]),
                        row(diff_sub_norm_g[l]), lam_init)
        x2 = _outproj(x2, o_r.reshape(b * seq, -1), o_d.reshape(b * seq, -1), z2,
                      ret_w_o[l].astype(BF16), diff_w_o[l].astype(BF16), w_out[l].astype(BF16))
    return x2.reshape(b, seq, d)
```

```python
import functools
import math

import jax
import jax.numpy as jnp
from jax import lax
from jax.experimental import pallas as pl
from jax.experimental.pallas import tpu as pltpu

D_MODEL = 1024
HEADS = 8
DK = 64
DV = 128
EPS = 1e-6
LANES = 128
N_IN = 9216
N_MAIN = 7168
N_T = 2048

COL_QR, COL_KR, COL_VR, COL_GR = 0, 4, 8, 16
COL_KD, COL_GD, COL_MR, COL_MD = 24, 32, 40, 48

IN_TM, IN_TN = 1024, 1024
T_TN = 1024
NORM_ROWS = 256
RET_CHUNK = 256
ATT_TQ = 512
ATT_TK = 256
OUT_TM = 512
NEG = -1e30
LOG2E = 1.4426950408889634
POS_COPIES = 3
SUM_ROWS = 16

F32 = jnp.float32
BF16 = jnp.bfloat16
_NT = (((1,), (1,)), ((), ()))


def _inproj_kernel(x_ref, g_ref, w_ref, z_ref, h_ref):
    @pl.when(pl.program_id(1) == 0)
    def _():
        def body(r, carry):
            rows = pl.ds(pl.multiple_of(r * NORM_ROWS, NORM_ROWS), NORM_ROWS)
            x = x_ref[rows, :]
            ms = jnp.mean(x * x, axis=-1, keepdims=True)
            h_ref[rows, :] = (x * lax.rsqrt(ms + EPS) * g_ref[...]).astype(BF16)
            return carry
        lax.fori_loop(0, IN_TM // NORM_ROWS, body, 0)

    z_ref[...] = jnp.dot(h_ref[...], w_ref[...], preferred_element_type=F32).astype(z_ref.dtype)


def _inproj(x2, g, w):
    m = x2.shape[0]
    return pl.pallas_call(
        _inproj_kernel,
        grid=(m // IN_TM, N_MAIN // IN_TN),
        in_specs=[
            pl.BlockSpec((IN_TM, D_MODEL), lambda i, j: (i, 0)),
            pl.BlockSpec((1, D_MODEL), lambda i, j: (0, 0)),
            pl.BlockSpec((D_MODEL, IN_TN), lambda i, j: (0, j)),
        ],
        out_specs=[
            pl.BlockSpec((IN_TM, IN_TN), lambda i, j: (i, j)),
            pl.BlockSpec((IN_TM, D_MODEL), lambda i, j: (i, 0)),
        ],
        out_shape=[
            jax.ShapeDtypeStruct((m, N_MAIN), BF16),
            jax.ShapeDtypeStruct((m, D_MODEL), BF16),
        ],
        compiler_params=pltpu.CompilerParams(dimension_semantics=("parallel", "arbitrary")),
        name="inproj",
    )(x2, g, w)


def _tproj_kernel(h_ref, wt_ref, o_ref):
    res = lax.dot_general(wt_ref[...], h_ref[...], _NT, preferred_element_type=F32).astype(o_ref.dtype)
    for c in range(IN_TM // ATT_TK):
        o_ref[c] = res[:, c * ATT_TK:(c + 1) * ATT_TK]


def _tproj(h2, wt, batch, seq):
    m = h2.shape[0]
    per_batch = seq // IN_TM
    sub = IN_TM // ATT_TK
    return pl.pallas_call(
        _tproj_kernel,
        grid=(m // IN_TM, N_T // T_TN),
        in_specs=[
            pl.BlockSpec((IN_TM, D_MODEL), lambda i, n: (i, 0)),
            pl.BlockSpec((T_TN, D_MODEL), lambda i, n: (n, 0)),
        ],
        out_specs=pl.BlockSpec((None, sub, T_TN, ATT_TK),
                               lambda i, n: (i // per_batch, i % per_batch, n, 0)),
        out_shape=jax.ShapeDtypeStruct((batch, seq // ATT_TK, N_T, ATT_TK), BF16),
        compiler_params=pltpu.CompilerParams(dimension_semantics=("parallel", "parallel")),
        name="tproj",
    )(h2, wt)


def _retention_kernel(lg_ref, q_ref, k_ref, v_ref, gate_ref, g_ref, o_ref,
                      dmat_ref, kdec_ref, qdec_ref, sdec_ref, smask_ref, state_ref):
    c = RET_CHUNK
    seq = q_ref.shape[0]
    pair = pl.program_id(1)
    lga = lg_ref[2 * pair]
    lgb = lg_ref[2 * pair + 1]
    scale = DK ** -0.5

    n = lax.broadcasted_iota(jnp.int32, (c, c), 0)
    mm = lax.broadcasted_iota(jnp.int32, (c, c), 1)
    rel = (n - mm).astype(F32)
    causal = rel >= 0.0
    relc = jnp.maximum(rel, 0.0)
    dmat_ref[0:c, :] = jnp.where(causal, jnp.exp(relc * lga), 0.0) * scale
    dmat_ref[c:2 * c, :] = jnp.where(causal, jnp.exp(relc * lgb), 0.0) * scale
    pos_k = lax.broadcasted_iota(jnp.int32, (c, LANES), 0).astype(F32)
    lane_k = lax.broadcasted_iota(jnp.int32, (c, LANES), 1)
    kdec_ref[...] = jnp.exp((c - 1.0 - pos_k) * jnp.where(lane_k < DK, lga, lgb)) * scale
    pos_q = lax.broadcasted_iota(jnp.int32, (c, 2 * DV), 0).astype(F32)
    lane_q = lax.broadcasted_iota(jnp.int32, (c, 2 * DV), 1)
    qdec_ref[...] = jnp.exp((pos_q + 1.0) * jnp.where(lane_q < DV, lga, lgb))
    srow = lax.broadcasted_iota(jnp.int32, (2 * DK, 2 * DV), 0)
    scol = lax.broadcasted_iota(jnp.int32, (2 * DK, 2 * DV), 1)
    in_a = (srow < DK) & (scol < DV)
    in_b = (srow >= DK) & (scol >= DV)
    smask_ref[...] = jnp.where(in_a | in_b, 1.0, 0.0)
    sdec_ref[...] = (jnp.where(in_a, jnp.exp(jnp.full((2 * DK, 2 * DV), c, F32) * lga), 0.0)
                     + jnp.where(in_b, jnp.exp(jnp.full((2 * DK, 2 * DV), c, F32) * lgb), 0.0))
    state_ref[...] = jnp.zeros_like(state_ref)

    def chunk(ci, carry):
        rows = pl.ds(pl.multiple_of(ci * c, c), c)
        qc = q_ref[rows, :]
        kc = k_ref[rows, :]
        vc = v_ref[rows, :]
        q32 = qc.astype(F32)
        lane = lax.broadcasted_iota(jnp.int32, (c, LANES), 1)
        qa = jnp.where(lane < DK, q32, 0.0).astype(BF16)
        qb = jnp.where(lane >= DK, q32, 0.0).astype(BF16)
        qs = jnp.concatenate([qa, qb], axis=0)
        sc = lax.dot_general(qs, kc, _NT, preferred_element_type=F32)
        pm = (sc * dmat_ref[...]).astype(BF16)
        intra_a = jnp.dot(pm[0:c], vc[:, 0:DV], preferred_element_type=F32)
        intra_b = jnp.dot(pm[c:2 * c], vc[:, DV:2 * DV], preferred_element_type=F32)
        st = state_ref[...]
        inter = jnp.dot(qc, st.astype(BF16), preferred_element_type=F32)
        out = jnp.concatenate([intra_a, intra_b], axis=1) + inter * qdec_ref[...]

        kd_t = (kc.astype(F32) * kdec_ref[...]).T.astype(BF16)
        upd = jnp.dot(kd_t, vc, preferred_element_type=F32)
        state_ref[...] = st * sdec_ref[...] + upd * smask_ref[...]

        g = g_ref[...]
        gate = gate_ref[rows, :].astype(F32)
        act = gate * jax.nn.sigmoid(gate)
        halves = []
        for hh in range(2):
            o = out[:, hh * DV:(hh + 1) * DV]
            ms = jnp.mean(o * o, axis=-1, keepdims=True)
            halves.append(o * lax.rsqrt(ms + EPS) * g[:, hh * DV:(hh + 1) * DV])
        o_ref[rows, :] = (jnp.concatenate(halves, axis=1) * act).astype(o_ref.dtype)
        return carry

    lax.fori_loop(0, seq // c, chunk, 0)


def _retention(z3, log_gamma, g):
    b, seq, _ = z3.shape
    c = RET_CHUNK
    return pl.pallas_call(
        _retention_kernel,
        grid=(b, HEADS // 2),
        in_specs=[
            pl.BlockSpec(memory_space=pltpu.SMEM),
            pl.BlockSpec((None, seq, LANES), lambda i, p: (i, 0, COL_QR + p)),
            pl.BlockSpec((None, seq, LANES), lambda i, p: (i, 0, COL_KR + p)),
            pl.BlockSpec((None, seq, 2 * DV), lambda i, p: (i, 0, COL_VR // 2 + p)),
            pl.BlockSpec((None, seq, 2 * DV), lambda i, p: (i, 0, COL_GR // 2 + p)),
            pl.BlockSpec((1, 2 * DV), lambda i, p: (0, p)),
        ],
        out_specs=pl.BlockSpec((None, seq, 2 * DV), lambda i, p: (i, 0, p)),
        out_shape=jax.ShapeDtypeStruct((b, seq, HEADS * DV), BF16),
        scratch_shapes=[
            pltpu.VMEM((2 * c, c), F32),
            pltpu.VMEM((c, LANES), F32),
            pltpu.VMEM((c, 2 * DV), F32),
            pltpu.VMEM((2 * DK, 2 * DV), F32),
            pltpu.VMEM((2 * DK, 2 * DV), F32),
            pltpu.VMEM((2 * DK, 2 * DV), F32),
        ],
        compiler_params=pltpu.CompilerParams(dimension_semantics=("parallel", "parallel")),
        name="retention",
    )(log_gamma, z3, z3, z3, z3, g)


def _group_rms_norm(x, g):
    lane = lax.broadcasted_iota(jnp.int32, x.shape, 1)
    lo = lane < DK
    sq = x * x
    s_lo = jnp.sum(jnp.where(lo, sq, 0.0), axis=-1, keepdims=True)
    s_hi = jnp.sum(jnp.where(lo, 0.0, sq), axis=-1, keepdims=True)
    ms = jnp.where(lo, s_lo, s_hi) * (1.0 / DK)
    return x * lax.rsqrt(ms + EPS) * g


def _diffattn_kernel(slope_ref, qt_ref, k_ref, vt_ref, gate_ref, gq_ref, gk_ref,
                     lq1_ref, lk1_ref, lq2_ref, lk2_ref, gs_ref, o_ref,
                     kn_ref, qs_ref, sa_ref, sb_ref, m_ref, acc_ref, *, lam_init):
    tq, tk = ATT_TQ, ATT_TK
    seq = k_ref.shape[0]
    sub = tq // tk
    slope = slope_ref[pl.program_id(1)]
    lam = (jnp.exp(jnp.sum(lq1_ref[...] * lk1_ref[...], axis=-1, keepdims=True))
           - jnp.exp(jnp.sum(lq2_ref[...] * lk2_ref[...], axis=-1, keepdims=True))
           + lam_init)

    def knorm(r, carry):
        rows = pl.ds(pl.multiple_of(r * NORM_ROWS, NORM_ROWS), NORM_ROWS)
        kn_ref[rows, 0:LANES] = _group_rms_norm(k_ref[rows, :].astype(F32), gk_ref[...]).astype(BF16)
        pos = r * NORM_ROWS + lax.broadcasted_iota(jnp.int32, (NORM_ROWS, LANES), 0)
        lane = lax.broadcasted_iota(jnp.int32, (NORM_ROWS, LANES), 1)
        lo = pos & (tk - 1)
        part = jnp.where(lane >= 2 * POS_COPIES, 0, jnp.where((lane & 1) == 0, lo, pos - lo))
        kn_ref[rows, LANES:2 * LANES] = part.astype(F32).astype(BF16)
        return carry
    lax.fori_loop(0, seq // NORM_ROWS, knorm, 0)

    c_full = jnp.full((LANES, 2 * tq), slope * LOG2E, F32)
    c0 = c_full.astype(BF16).astype(F32)
    c1 = (c_full - c0).astype(BF16).astype(F32)
    c2 = c_full - c0 - c1
    piece = lax.broadcasted_iota(jnp.int32, (LANES, 2 * tq), 0) >> 1
    qs_ref[LANES:2 * LANES, :] = jnp.where(
        piece == 0, c0, jnp.where(piece == 1, c1, jnp.where(piece == 2, c2, 0.0))).astype(BF16)
    ones_rows = jnp.ones((SUM_ROWS, tk), BF16)

    def q_tile(qi, carry):
        qt = jnp.concatenate([qt_ref[sub * qi + c] for c in range(sub)], axis=1).astype(F32)
        gq = gq_ref[...]
        zeros = jnp.zeros((DK, tq), F32)
        halves = []
        for hh in range(2):
            x = qt[hh * DK:(hh + 1) * DK]
            ms = jnp.mean(x * x, axis=0, keepdims=True)
            halves.append(x * lax.rsqrt(ms + EPS) * gq[hh * DK:(hh + 1) * DK] * (DK ** -0.5 * LOG2E))
        qs_ref[0:DK, :] = jnp.concatenate([halves[0], zeros], axis=1).astype(BF16)
        qs_ref[DK:2 * DK, :] = jnp.concatenate([zeros, halves[1]], axis=1).astype(BF16)
        m_ref[...] = jnp.full_like(m_ref, NEG)
        acc_ref[...] = jnp.zeros_like(acc_ref)

        def scores(j, dst):
            keys = pl.ds(pl.multiple_of(j * tk, tk), tk)
            dst[...] = jnp.dot(kn_ref[keys, :], qs_ref[...], preferred_element_type=F32)

        def absorb(j, src, diag):
            s = src[...]
            if diag is not None:
                key = lax.broadcasted_iota(jnp.int32, (tk, 2 * tq), 0) + diag * tk
                qry = lax.broadcasted_iota(jnp.int32, (tk, 2 * tq), 1) & (tq - 1)
                s = jnp.where(key <= qry, s, NEG)
            m_old = m_ref[...]
            m_new = jnp.maximum(m_old, jnp.max(s, axis=0, keepdims=True))
            alpha = jnp.exp2(m_old - m_new)
            p = jnp.exp2(s - m_new).astype(BF16)
            v_aug = jnp.concatenate([vt_ref[j], ones_rows], axis=0)
            acc_ref[...] = alpha * acc_ref[...] + jnp.dot(v_aug, p, preferred_element_type=F32)
            m_ref[...] = m_new

        scores(0, sa_ref)

        def full_pair(mi, c2):
            j = 2 * mi
            scores(j + 1, sb_ref)
            absorb(j, sa_ref, None)
            scores(j + 2, sa_ref)
            absorb(j + 1, sb_ref, None)
            return c2
        lax.fori_loop(0, qi * (sub // 2), full_pair, 0)
        first_diag = sub * qi
        scores(first_diag + 1, sb_ref)
        absorb(first_diag, sa_ref, 0)
        absorb(first_diag + 1, sb_ref, 1)

        inv = 1.0 / acc_ref[DV:DV + 1, :]
        acc = acc_ref[0:DV, :] * inv
        ot = acc[:, 0:tq] - lam * acc[:, tq:2 * tq]
        ms = jnp.mean(ot * ot, axis=0, keepdims=True)
        on = (ot * lax.rsqrt(ms + EPS)).T
        rows = pl.ds(pl.multiple_of(qi * tq, tq), tq)
        gate = gate_ref[rows, :].astype(F32)
        o_ref[rows, :] = (on * gs_ref[...] * (1.0 - lam_init)
                          * (gate * jax.nn.sigmoid(gate))).astype(o_ref.dtype)
        return carry

    lax.fori_loop(0, seq // tq, q_tile, 0)


def _diffattn(z3, t4, slopes, gq, gk, lq1, lk1, lq2, lk2, gs, lam_init):
    b, seq, _ = z3.shape
    tq, tk = ATT_TQ, ATT_TK
    assert tq == 2 * tk
    head_rows = lambda col: pl.BlockSpec((None, seq, LANES), lambda i, h: (i, 0, col + h))
    head_t = lambda off: pl.BlockSpec((None, seq // tk, LANES, tk), lambda i, h: (i, 0, off + h, 0))
    vec = lambda width: pl.BlockSpec((1, width), lambda i, h: (0, 0))
    return pl.pallas_call(
        functools.partial(_diffattn_kernel, lam_init=lam_init),
        grid=(b, HEADS),
        in_specs=[
            pl.BlockSpec(memory_space=pltpu.SMEM),
            head_t(0), head_rows(COL_KD), head_t(HEADS), head_rows(COL_GD),
            pl.BlockSpec((LANES, 1), lambda i, h: (0, 0)),
            vec(LANES), vec(DK), vec(DK), vec(DK), vec(DK),
            pl.BlockSpec((1, DV), lambda i, h: (0, h)),
        ],
        out_specs=pl.BlockSpec((None, seq, DV), lambda i, h: (i, 0, h)),
        out_shape=jax.ShapeDtypeStruct((b, seq, HEADS * DV), BF16),
        scratch_shapes=[
            pltpu.VMEM((seq, 2 * LANES), BF16),
            pltpu.VMEM((2 * LANES, 2 * tq), BF16),
            pltpu.VMEM((tk, 2 * tq), F32),
            pltpu.VMEM((tk, 2 * tq), F32),
            pltpu.VMEM((1, 2 * tq), F32),
            pltpu.VMEM((DV + SUM_ROWS, 2 * tq), F32),
        ],
        compiler_params=pltpu.CompilerParams(dimension_semantics=("parallel", "parallel")),
        name="diffattn",
    )(slopes, t4, z3, t4, z3, gq, gk, lq1, lk1, lq2, lk2, gs)


def _outproj_kernel(x_ref, or_ref, od_ref, mr_ref, md_ref, wr_ref, wd_ref, wo_ref, out_ref):
    y_r = jnp.dot(or_ref[...], wr_ref[...], preferred_element_type=F32)
    y_d = jnp.dot(od_ref[...], wd_ref[...], preferred_element_type=F32)
    merged = (jax.nn.sigmoid(mr_ref[...].astype(F32)) * y_r
              + jax.nn.sigmoid(md_ref[...].astype(F32)) * y_d)
    out_ref[...] = x_ref[...] + jnp.dot(merged.astype(BF16), wo_ref[...], preferred_element_type=F32)


def _outproj(x2, o_r, o_d, z2, w_r, w_d, w_o):
    m = x2.shape[0]
    rows = lambda col: pl.BlockSpec((OUT_TM, D_MODEL), lambda i: (i, col))
    weight = pl.BlockSpec((D_MODEL, D_MODEL), lambda i: (0, 0))
    return pl.pallas_call(
        _outproj_kernel,
        grid=(m // OUT_TM,),
        in_specs=[rows(0), rows(0), rows(0),
                  rows(COL_MR * LANES // D_MODEL), rows(COL_MD * LANES // D_MODEL),
                  weight, weight, weight],
        out_specs=rows(0),
        out_shape=jax.ShapeDtypeStruct((m, D_MODEL), F32),
        compiler_params=pltpu.CompilerParams(dimension_semantics=("parallel",)),
        name="outproj",
    )(x2, o_r, o_d, z2, z2, w_r, w_d, w_o)


def kernel(x, norm_g, w_in, ret_norm_g, ret_w_o, diff_q_norm_g, diff_k_norm_g,
           diff_lq1, diff_lk1, diff_lq2, diff_lk2, diff_sub_norm_g, diff_w_o, w_out):
    b, seq, d = x.shape
    depth = norm_g.shape[0]
    assert d == D_MODEL and w_in.shape[-1] == N_IN
    assert seq % ATT_TQ == 0 and seq % RET_CHUNK == 0 and seq % IN_TM == 0

    assert seq <= ATT_TK * 256
    slopes = jnp.exp2(-8.0 * jnp.arange(1, HEADS + 1, dtype=F32) / HEADS)
    log_gamma = jnp.log1p(-jnp.exp2(-5.0 - jnp.arange(HEADS, dtype=F32)))
    row = lambda v: v.reshape(1, -1).astype(F32)

    x2 = x.reshape(b * seq, d)
    for l in range(depth):
        lam_init = 0.8 - 0.6 * math.exp(-0.3 * l)
        w = w_in[l]
        w_main = jnp.concatenate([w[:, 0:3072], w[:, 4096:5120], w[:, 6144:9216]], axis=1).astype(BF16)
        w_t = jnp.concatenate([w[:, 3072:4096], w[:, 5120:6144]], axis=1).T.astype(BF16)
        z2, h2 = _inproj(x2, row(norm_g[l]), w_main)
        t4 = _tproj(h2, w_t, b, seq)
        z3 = z2.reshape(b, seq, N_MAIN)
        o_r = _retention(z3, log_gamma, row(ret_norm_g[l]))
        gq_col = jnp.concatenate([diff_q_norm_g[l], diff_q_norm_g[l]]).reshape(-1, 1).astype(F32)
        gk_row = jnp.concatenate([diff_k_norm_g[l], diff_k_norm_g[l]]).reshape(1, -1).astype(F32)
        o_d = _diffattn(z3, t4, slopes, gq_col, gk_row,
                        row(diff_lq1[l]), row(diff_lk1[l]), row(diff_lq2[l]), row(diff_lk2[l]),
                        row(diff_sub_norm_g[l]), lam_init)
        x2 = _outproj(x2, o_r.reshape(b * seq, -1), o_d.reshape(b * seq, -1), z2,
                      ret_w_o[l].astype(BF16), diff_w_o[l].astype(BF16), w_out[l].astype(BF16))
    return x2.reshape(b, seq, d)
```

```python
import functools
import math

import jax
import jax.numpy as jnp
from jax import lax
from jax.experimental import pallas as pl
from jax.experimental.pallas import tpu as pltpu

D_MODEL = 1024
HEADS = 8
DK = 64
DV = 128
EPS = 1e-6
LANES = 128
N_IN = 9216
N_MAIN = 7168
N_T = 2048

COL_QR, COL_KR, COL_VR, COL_GR = 0, 4, 8, 16
COL_KD, COL_GD, COL_MR, COL_MD = 24, 32, 40, 48

IN_TM, IN_TN = 512, 1024
IN_VMEM_BYTES = (2 * D_MODEL * (N_MAIN + N_T) + 2 * IN_TM * D_MODEL * 4 + 2 * IN_TM * (N_MAIN + N_T) * 2
                 + IN_TM * D_MODEL * 2 + 2 * IN_TM * IN_TN * 4 + (1 << 20))
NORM_ROWS = 256
RET_CHUNK = 256
RET_ROWS = 1024
ATT_TQ = 512
ATT_TK = 256
OUT_TM = 512
NEG = -1e30
LOG2E = 1.4426950408889634
POS_COPIES = 3
SUM_ROWS = 16

F32 = jnp.float32
BF16 = jnp.bfloat16
_NT = (((1,), (1,)), ((), ()))


def _silu(v):
    return v * jax.nn.sigmoid(v)


_CHUNK_ACT = (None, None, _silu, None, _silu, jax.nn.sigmoid, jax.nn.sigmoid)


def _inproj_kernel(x_ref, g_ref, w_ref, wt_ref, z_ref, t_ref, h_ref):
    for r in range(IN_TM // NORM_ROWS):
        rows = slice(r * NORM_ROWS, (r + 1) * NORM_ROWS)
        x = x_ref[rows, :]
        ms = jnp.mean(x * x, axis=-1, keepdims=True)
        h_ref[rows, :] = (x * lax.rsqrt(ms + EPS) * g_ref[...]).astype(BF16)
    h = h_ref[...]
    for n, act in enumerate(_CHUNK_ACT):
        cols = slice(n * IN_TN, (n + 1) * IN_TN)
        y = jnp.dot(h, w_ref[:, cols], preferred_element_type=F32)
        if act is not None:
            y = act(y)
        z_ref[:, cols] = y.astype(z_ref.dtype)
    t = lax.dot_general(wt_ref[...], h, _NT, preferred_element_type=F32).astype(t_ref.dtype)
    for c in range(IN_TM // ATT_TK):
        t_ref[c] = t[:, c * ATT_TK:(c + 1) * ATT_TK]


def _inproj(x2, g, w, wt, batch, seq):
    m = x2.shape[0]
    per_batch = seq // IN_TM
    sub = IN_TM // ATT_TK
    resident = dict(pipeline_mode=pl.Buffered(1))
    return pl.pallas_call(
        _inproj_kernel,
        grid=(m // IN_TM,),
        in_specs=[
            pl.BlockSpec((IN_TM, D_MODEL), lambda i: (i, 0)),
            pl.BlockSpec((1, D_MODEL), lambda i: (0, 0)),
            pl.BlockSpec((D_MODEL, N_MAIN), lambda i: (0, 0), **resident),
            pl.BlockSpec((N_T, D_MODEL), lambda i: (0, 0), **resident),
        ],
        out_specs=[
            pl.BlockSpec((IN_TM, N_MAIN), lambda i: (i, 0)),
            pl.BlockSpec((None, sub, N_T, ATT_TK), lambda i: (i // per_batch, i % per_batch, 0, 0)),
        ],
        out_shape=[
            jax.ShapeDtypeStruct((m, N_MAIN), BF16),
            jax.ShapeDtypeStruct((batch, seq // ATT_TK, N_T, ATT_TK), BF16),
        ],
        scratch_shapes=[pltpu.VMEM((IN_TM, D_MODEL), BF16)],
        compiler_params=pltpu.CompilerParams(dimension_semantics=("parallel",),
                                             vmem_limit_bytes=IN_VMEM_BYTES),
        name="inproj",
    )(x2, g, w, wt)


def _retention_kernel(lg_ref, q_ref, k_ref, v_ref, gate_ref, g_ref, o_ref,
                      dmat_ref, kdec_ref, qdec_ref, sdec_ref, smask_ref, state_ref):
    c = RET_CHUNK
    n_pairs = HEADS // 2
    scale = DK ** -0.5

    @pl.when(pl.program_id(1) == 0)
    def _():
        n = lax.broadcasted_iota(jnp.int32, (c, c), 0)
        mm = lax.broadcasted_iota(jnp.int32, (c, c), 1)
        rel = (n - mm).astype(F32)
        causal = rel >= 0.0
        relc = jnp.maximum(rel, 0.0)
        pos_k = lax.broadcasted_iota(jnp.int32, (c, LANES), 0).astype(F32)
        lane_k = lax.broadcasted_iota(jnp.int32, (c, LANES), 1)
        pos_q = lax.broadcasted_iota(jnp.int32, (c, 2 * DV), 0).astype(F32)
        lane_q = lax.broadcasted_iota(jnp.int32, (c, 2 * DV), 1)
        srow = lax.broadcasted_iota(jnp.int32, (2 * DK, 2 * DV), 0)
        scol = lax.broadcasted_iota(jnp.int32, (2 * DK, 2 * DV), 1)
        in_a = (srow < DK) & (scol < DV)
        in_b = (srow >= DK) & (scol >= DV)
        smask_ref[...] = jnp.where(in_a | in_b, 1.0, 0.0)
        full_c = jnp.full((2 * DK, 2 * DV), c, F32)
        for pp in range(n_pairs):
            lga = lg_ref[2 * pp]
            lgb = lg_ref[2 * pp + 1]
            dmat_ref[pp, 0:c, :] = jnp.where(causal, jnp.exp(relc * lga), 0.0) * scale
            dmat_ref[pp, c:2 * c, :] = jnp.where(causal, jnp.exp(relc * lgb), 0.0) * scale
            kdec_ref[pp] = jnp.exp((c - 1.0 - pos_k) * jnp.where(lane_k < DK, lga, lgb)) * scale
            qdec_ref[pp] = jnp.exp((pos_q + 1.0) * jnp.where(lane_q < DV, lga, lgb))
            sdec_ref[pp] = (jnp.where(in_a, jnp.exp(full_c * lga), 0.0)
                            + jnp.where(in_b, jnp.exp(full_c * lgb), 0.0))
        state_ref[...] = jnp.zeros_like(state_ref)

    def chunk(ci, carry):
        rows = pl.ds(pl.multiple_of(ci * c, c), c)
        lane = lax.broadcasted_iota(jnp.int32, (c, LANES), 1)
        for pp in range(n_pairs):
            qk_cols = slice(pp * LANES, (pp + 1) * LANES)
            v_cols = slice(pp * 2 * DV, (pp + 1) * 2 * DV)
            qc = q_ref[rows, qk_cols]
            kc = k_ref[rows, qk_cols]
            vc = v_ref[rows, v_cols]
            q32 = qc.astype(F32)
            qa = jnp.where(lane < DK, q32, 0.0).astype(BF16)
            qb = jnp.where(lane >= DK, q32, 0.0).astype(BF16)
            qs = jnp.concatenate([qa, qb], axis=0)
            sc = lax.dot_general(qs, kc, _NT, preferred_element_type=F32)
            pm = (sc * dmat_ref[pp]).astype(BF16)
            intra_a = jnp.dot(pm[0:c], vc[:, 0:DV], preferred_element_type=F32)
            intra_b = jnp.dot(pm[c:2 * c], vc[:, DV:2 * DV], preferred_element_type=F32)
            st = state_ref[pp]
            inter = jnp.dot(qc, st.astype(BF16), preferred_element_type=F32)
            out = jnp.concatenate([intra_a, intra_b], axis=1) + inter * qdec_ref[pp]

            kd_t = (kc.astype(F32) * kdec_ref[pp]).T.astype(BF16)
            upd = jnp.dot(kd_t, vc, preferred_element_type=F32)
            state_ref[pp] = st * sdec_ref[pp] + upd * smask_ref[...]

            g = g_ref[:, v_cols]
            act = gate_ref[rows, v_cols].astype(F32)
            halves = []
            for hh in range(2):
                o = out[:, hh * DV:(hh + 1) * DV]
                ms = jnp.mean(o * o, axis=-1, keepdims=True)
                halves.append(o * lax.rsqrt(ms + EPS) * g[:, hh * DV:(hh + 1) * DV])
            o_ref[rows, v_cols] = (jnp.concatenate(halves, axis=1) * act).astype(o_ref.dtype)
        return carry

    lax.fori_loop(0, RET_ROWS // c, chunk, 0)


def _retention(z3, log_gamma, g):
    b, seq, _ = z3.shape
    c = RET_CHUNK
    n_pairs = HEADS // 2
    qk_w = n_pairs * LANES
    v_w = HEADS * DV
    return pl.pallas_call(
        _retention_kernel,
        grid=(b, seq // RET_ROWS),
        in_specs=[
            pl.BlockSpec(memory_space=pltpu.SMEM),
            pl.BlockSpec((None, RET_ROWS, qk_w), lambda i, s: (i, s, COL_QR * LANES // qk_w)),
            pl.BlockSpec((None, RET_ROWS, qk_w), lambda i, s: (i, s, COL_KR * LANES // qk_w)),
            pl.BlockSpec((None, RET_ROWS, v_w), lambda i, s: (i, s, COL_VR * LANES // v_w)),
            pl.BlockSpec((None, RET_ROWS, v_w), lambda i, s: (i, s, COL_GR * LANES // v_w)),
            pl.BlockSpec((1, v_w), lambda i, s: (0, 0)),
        ],
        out_specs=pl.BlockSpec((None, RET_ROWS, v_w), lambda i, s: (i, s, 0)),
        out_shape=jax.ShapeDtypeStruct((b, seq, v_w), BF16),
        scratch_shapes=[
            pltpu.VMEM((n_pairs, 2 * c, c), F32),
            pltpu.VMEM((n_pairs, c, LANES), F32),
            pltpu.VMEM((n_pairs, c, 2 * DV), F32),
            pltpu.VMEM((n_pairs, 2 * DK, 2 * DV), F32),
            pltpu.VMEM((2 * DK, 2 * DV), F32),
            pltpu.VMEM((n_pairs, 2 * DK, 2 * DV), F32),
        ],
        compiler_params=pltpu.CompilerParams(dimension_semantics=("parallel", "arbitrary")),
        name="retention",
    )(log_gamma, z3, z3, z3, z3, g)


def _diffattn_kernel(slope_ref, qt_ref, k_ref, vt_ref, gate_ref, gq_ref, gk_ref,
                     lq1_ref, lk1_ref, lq2_ref, lk2_ref, gs_ref, o_ref,
                     kn_ref, qs_ref, sa_ref, sb_ref, m_ref, acc_ref, *, lam_init):
    tq, tk = ATT_TQ, ATT_TK
    seq = k_ref.shape[0]
    sub = tq // tk
    slope = slope_ref[pl.program_id(1)]
    lam = (jnp.exp(jnp.sum(lq1_ref[...] * lk1_ref[...], axis=-1, keepdims=True))
           - jnp.exp(jnp.sum(lq2_ref[...] * lk2_ref[...], axis=-1, keepdims=True))
           + lam_init)
    same_map = ((lax.broadcasted_iota(jnp.int32, (LANES, LANES), 0) < DK)
                == (lax.broadcasted_iota(jnp.int32, (LANES, LANES), 1) < DK)).astype(F32).astype(BF16)

    def knorm(r, carry):
        rows = pl.ds(pl.multiple_of(r * NORM_ROWS, NORM_ROWS), NORM_ROWS)
        k32 = k_ref[rows, :].astype(F32)
        sq = k32 * k32
        hi = sq.astype(BF16)
        lo = (sq - hi.astype(F32)).astype(BF16)
        ss = (jnp.dot(hi, same_map, preferred_element_type=F32)
              + jnp.dot(lo, same_map, preferred_element_type=F32))
        kn_ref[rows, 0:LANES] = (k32 * lax.rsqrt(ss * (1.0 / DK) + EPS) * gk_ref[...]).astype(BF16)
        pos = r * NORM_ROWS + lax.broadcasted_iota(jnp.int32, (NORM_ROWS, LANES), 0)
        lane = lax.broadcasted_iota(jnp.int32, (NORM_ROWS, LANES), 1)
        lo = pos & (tk - 1)
        part = jnp.where(lane >= 2 * POS_COPIES, 0, jnp.where((lane & 1) == 0, lo, pos - lo))
        kn_ref[rows, LANES:2 * LANES] = part.astype(F32).astype(BF16)
        return carry
    lax.fori_loop(0, seq // NORM_ROWS, knorm, 0, unroll=4)

    c_full = jnp.full((LANES, 2 * tq), slope * LOG2E, F32)
    c0 = c_full.astype(BF16).astype(F32)
    c1 = (c_full - c0).astype(BF16).astype(F32)
    c2 = c_full - c0 - c1
    piece = lax.broadcasted_iota(jnp.int32, (LANES, 2 * tq), 0) >> 1
    qs_ref[LANES:2 * LANES, :] = jnp.where(
        piece == 0, c0, jnp.where(piece == 1, c1, jnp.where(piece == 2, c2, 0.0))).astype(BF16)
    ones_rows = jnp.ones((SUM_ROWS, tk), BF16)

    def q_tile(qi, carry):
        qt = jnp.concatenate([qt_ref[sub * qi + c] for c in range(sub)], axis=1).astype(F32)
        gq = gq_ref[...]
        zeros = jnp.zeros((DK, tq), F32)
        halves = []
        for hh in range(2):
            x = qt[hh * DK:(hh + 1) * DK]
            ms = jnp.mean(x * x, axis=0, keepdims=True)
            halves.append(x * lax.rsqrt(ms + EPS) * gq[hh * DK:(hh + 1) * DK] * (DK ** -0.5 * LOG2E))
        qs_ref[0:DK, :] = jnp.concatenate([halves[0], zeros], axis=1).astype(BF16)
        qs_ref[DK:2 * DK, :] = jnp.concatenate([zeros, halves[1]], axis=1).astype(BF16)
        def scores(j, dst):
            keys = pl.ds(pl.multiple_of(j * tk, tk), tk)
            dst[...] = jnp.dot(kn_ref[keys, :], qs_ref[...], preferred_element_type=F32)

        def absorb(j, src, diag=None, first=False):
            s = src[...]
            if diag is not None:
                key = lax.broadcasted_iota(jnp.int32, (tk, 2 * tq), 0) + diag * tk
                qry = lax.broadcasted_iota(jnp.int32, (tk, 2 * tq), 1) & (tq - 1)
                s = jnp.where(key <= qry, s, NEG)
            m_new = jnp.max(s, axis=0, keepdims=True)
            if not first:
                m_old = m_ref[...]
                m_new = jnp.maximum(m_old, m_new)
                alpha = jnp.exp2(m_old - m_new)
            p = jnp.exp2(s - m_new).astype(BF16)
            v_aug = jnp.concatenate([vt_ref[j], ones_rows], axis=0)
            pv = jnp.dot(v_aug, p, preferred_element_type=F32)
            acc_ref[...] = pv if first else alpha * acc_ref[...] + pv
            m_ref[...] = m_new

        first_diag = sub * qi
        scores(first_diag, sa_ref)
        scores(first_diag + 1, sb_ref)
        absorb(first_diag, sa_ref, diag=0, first=True)
        scores(0, sa_ref)
        absorb(first_diag + 1, sb_ref, diag=1)

        def full_pair(j, last=False):
            scores(j + 1, sb_ref)
            absorb(j, sa_ref)
            if not last:
                scores(j + 2, sa_ref)
            absorb(j + 1, sb_ref)

        n_loop_pairs = qi - 1
        n_quads = n_loop_pairs >> 1

        def quad(mi, c2):
            full_pair(4 * mi)
            full_pair(4 * mi + 2)
            return c2
        lax.fori_loop(0, n_quads, quad, 0)

        @pl.when((qi > 0) & ((n_loop_pairs & 1) == 1))
        def _():
            full_pair(4 * n_quads)

        @pl.when(qi > 0)
        def _():
            full_pair(2 * n_loop_pairs, last=True)

        inv = 1.0 / acc_ref[DV:DV + 1, :]
        acc = acc_ref[0:DV, :] * inv
        ot = acc[:, 0:tq] - lam * acc[:, tq:2 * tq]
        ms = jnp.mean(ot * ot, axis=0, keepdims=True)
        on = (ot * lax.rsqrt(ms + EPS)).T
        rows = pl.ds(pl.multiple_of(qi * tq, tq), tq)
        act = gate_ref[rows, :].astype(F32)
        o_ref[rows, :] = (on * gs_ref[...] * (1.0 - lam_init) * act).astype(o_ref.dtype)
        return carry

    lax.fori_loop(0, seq // tq, q_tile, 0)


def _diffattn(z3, t4, slopes, gq, gk, lq1, lk1, lq2, lk2, gs, lam_init):
    b, seq, _ = z3.shape
    tq, tk = ATT_TQ, ATT_TK
    assert tq == 2 * tk
    head_rows = lambda col: pl.BlockSpec((None, seq, LANES), lambda i, h: (i, 0, col + h))
    head_t = lambda off: pl.BlockSpec((None, seq // tk, LANES, tk), lambda i, h: (i, 0, off + h, 0))
    vec = lambda width: pl.BlockSpec((1, width), lambda i, h: (0, 0))
    return pl.pallas_call(
        functools.partial(_diffattn_kernel, lam_init=lam_init),
        grid=(b, HEADS),
        in_specs=[
            pl.BlockSpec(memory_space=pltpu.SMEM),
            head_t(0), head_rows(COL_KD), head_t(HEADS), head_rows(COL_GD),
            pl.BlockSpec((LANES, 1), lambda i, h: (0, 0)),
            vec(LANES), vec(DK), vec(DK), vec(DK), vec(DK),
            pl.BlockSpec((1, DV), lambda i, h: (0, h)),
        ],
        out_specs=pl.BlockSpec((None, seq, DV), lambda i, h: (i, 0, h)),
        out_shape=jax.ShapeDtypeStruct((b, seq, HEADS * DV), BF16),
        scratch_shapes=[
            pltpu.VMEM((seq, 2 * LANES), BF16),
            pltpu.VMEM((2 * LANES, 2 * tq), BF16),
            pltpu.VMEM((tk, 2 * tq), F32),
            pltpu.VMEM((tk, 2 * tq), F32),
            pltpu.VMEM((1, 2 * tq), F32),
            pltpu.VMEM((DV + SUM_ROWS, 2 * tq), F32),
        ],
        compiler_params=pltpu.CompilerParams(dimension_semantics=("parallel", "parallel")),
        name="diffattn",
    )(slopes, t4, z3, t4, z3, gq, gk, lq1, lk1, lq2, lk2, gs)


def _outproj_kernel(x_ref, or_ref, od_ref, mr_ref, md_ref, wr_ref, wd_ref, wo_ref, out_ref):
    y_r = jnp.dot(or_ref[...], wr_ref[...], preferred_element_type=F32)
    y_d = jnp.dot(od_ref[...], wd_ref[...], preferred_element_type=F32)
    merged = mr_ref[...].astype(F32) * y_r + md_ref[...].astype(F32) * y_d
    out_ref[...] = x_ref[...] + jnp.dot(merged.astype(BF16), wo_ref[...], preferred_element_type=F32)


def _outproj(x2, o_r, o_d, z2, w_r, w_d, w_o):
    m = x2.shape[0]
    rows = lambda col: pl.BlockSpec((OUT_TM, D_MODEL), lambda i: (i, col))
    weight = pl.BlockSpec((D_MODEL, D_MODEL), lambda i: (0, 0))
    return pl.pallas_call(
        _outproj_kernel,
        grid=(m // OUT_TM,),
        in_specs=[rows(0), rows(0), rows(0),
                  rows(COL_MR * LANES // D_MODEL), rows(COL_MD * LANES // D_MODEL),
                  weight, weight, weight],
        out_specs=rows(0),
        out_shape=jax.ShapeDtypeStruct((m, D_MODEL), F32),
        compiler_params=pltpu.CompilerParams(dimension_semantics=("parallel",)),
        name="outproj",
    )(x2, o_r, o_d, z2, z2, w_r, w_d, w_o)


def kernel(x, norm_g, w_in, ret_norm_g, ret_w_o, diff_q_norm_g, diff_k_norm_g,
           diff_lq1, diff_lk1, diff_lq2, diff_lk2, diff_sub_norm_g, diff_w_o, w_out):
    b, seq, d = x.shape
    depth = norm_g.shape[0]
    assert d == D_MODEL and w_in.shape[-1] == N_IN
    assert seq % ATT_TQ == 0 and seq % RET_ROWS == 0 and RET_ROWS % RET_CHUNK == 0 and seq % IN_TM == 0

    assert seq <= ATT_TK * 256
    slopes = jnp.exp2(-8.0 * jnp.arange(1, HEADS + 1, dtype=F32) / HEADS)
    log_gamma = jnp.log1p(-jnp.exp2(-5.0 - jnp.arange(HEADS, dtype=F32)))
    row = lambda v: v.reshape(1, -1).astype(F32)

    x2 = x.reshape(b * seq, d)
    for l in range(depth):
        lam_init = 0.8 - 0.6 * math.exp(-0.3 * l)
        w = w_in[l]
        w_main = jnp.concatenate([w[:, 0:3072], w[:, 4096:5120], w[:, 6144:9216]], axis=1).astype(BF16)
        w_t = jnp.concatenate([w[:, 3072:4096], w[:, 5120:6144]], axis=1).T.astype(BF16)
        z2, t4 = _inproj(x2, row(norm_g[l]), w_main, w_t, b, seq)
        z3 = z2.reshape(b, seq, N_MAIN)
        o_r = _retention(z3, log_gamma, row(ret_norm_g[l]))
        gq_col = jnp.concatenate([diff_q_norm_g[l], diff_q_norm_g[l]]).reshape(-1, 1).astype(F32)
        gk_row = jnp.concatenate([diff_k_norm_g[l], diff_k_norm_g[l]]).reshape(1, -1).astype(F32)
        o_d = _diffattn(z3, t4, slopes, gq_col, gk_row,
                        row(diff_lq1[l]), row(diff_lk1[l]), row(diff_lq2[l]), row(diff_lk2[l]),
                        row(diff_sub_norm_g[l]), lam_init)
        x2 = _outproj(x2, o_r.reshape(b * seq, -1), o_d.reshape(b * seq, -1), z2,
                      ret_w_o[l].astype(BF16), diff_w_o[l].astype(BF16), w_out[l].astype(BF16))
    return x2.reshape(b, seq, d)
```

```python
import functools
import math

import jax
import jax.numpy as jnp
from jax import lax
from jax.experimental import pallas as pl
from jax.experimental.pallas import tpu as pltpu

D_MODEL = 1024
HEADS = 8
DK = 64
DV = 128
EPS = 1e-6
LANES = 128
N_IN = 9216
N_MAIN = 7168
N_T = 2048

COL_QR, COL_KR, COL_VR, COL_GR = 0, 4, 8, 16
COL_KD, COL_GD, COL_MR, COL_MD = 24, 32, 40, 48

IN_TM, IN_TN = 512, 1024
IN_VMEM_BYTES = (2 * D_MODEL * (N_MAIN + N_T) + 2 * IN_TM * D_MODEL * 4 + 2 * IN_TM * (N_MAIN + N_T) * 2
                 + IN_TM * D_MODEL * 2 + 2 * IN_TM * IN_TN * 4 + (1 << 20))
NORM_ROWS = 256
RET_CHUNK = 256
RET_ROWS = 1024
ATT_TQ = 1024
ATT_TK = 256
OUT_TM = 512
NEG = -1e30
LOG2E = 1.4426950408889634
POS_COPIES = 3
SUM_ROWS = 16

F32 = jnp.float32
BF16 = jnp.bfloat16
_NT = (((1,), (1,)), ((), ()))


def _silu(v):
    return v * jax.nn.sigmoid(v)


_CHUNK_ACT = (None, None, _silu, None, _silu, jax.nn.sigmoid, jax.nn.sigmoid)


def _inproj_kernel(x_ref, g_ref, wa_ref, wk_ref, wc_ref, wt_ref, z_ref, t_ref, h_ref):
    for r in range(IN_TM // NORM_ROWS):
        rows = slice(r * NORM_ROWS, (r + 1) * NORM_ROWS)
        x = x_ref[rows, :]
        ms = jnp.mean(x * x, axis=-1, keepdims=True)
        h_ref[rows, :] = (x * lax.rsqrt(ms + EPS) * g_ref[...]).astype(BF16)
    h = h_ref[...]
    w_chunks = [(wa_ref, 0), (wa_ref, 1), (wa_ref, 2), (wk_ref, 0), (wc_ref, 0), (wc_ref, 1), (wc_ref, 2)]
    for n, (act, (w_ref, k)) in enumerate(zip(_CHUNK_ACT, w_chunks)):
        y = jnp.dot(h, w_ref[:, k * IN_TN:(k + 1) * IN_TN], preferred_element_type=F32)
        if act is not None:
            y = act(y)
        z_ref[:, n * IN_TN:(n + 1) * IN_TN] = y.astype(z_ref.dtype)
    t = lax.dot_general(wt_ref[...], h, _NT, preferred_element_type=F32).astype(t_ref.dtype)
    for c in range(IN_TM // ATT_TK):
        t_ref[c] = t[:, c * ATT_TK:(c + 1) * ATT_TK]


def _inproj(x2, g, w, wt, batch, seq):
    m = x2.shape[0]
    per_batch = seq // IN_TM
    sub = IN_TM // ATT_TK
    resident = dict(pipeline_mode=pl.Buffered(1))
    return pl.pallas_call(
        _inproj_kernel,
        grid=(m // IN_TM,),
        in_specs=[
            pl.BlockSpec((IN_TM, D_MODEL), lambda i: (i, 0)),
            pl.BlockSpec((1, D_MODEL), lambda i: (0, 0)),
            pl.BlockSpec((D_MODEL, 3 * IN_TN), lambda i: (0, 0), **resident),
            pl.BlockSpec((D_MODEL, IN_TN), lambda i: (0, 4), **resident),
            pl.BlockSpec((D_MODEL, 3 * IN_TN), lambda i: (0, 2), **resident),
            pl.BlockSpec((N_T, D_MODEL), lambda i: (0, 0), **resident),
        ],
        out_specs=[
            pl.BlockSpec((IN_TM, N_MAIN), lambda i: (i, 0)),
            pl.BlockSpec((None, sub, N_T, ATT_TK), lambda i: (i // per_batch, i % per_batch, 0, 0)),
        ],
        out_shape=[
            jax.ShapeDtypeStruct((m, N_MAIN), BF16),
            jax.ShapeDtypeStruct((batch, seq // ATT_TK, N_T, ATT_TK), BF16),
        ],
        scratch_shapes=[pltpu.VMEM((IN_TM, D_MODEL), BF16)],
        compiler_params=pltpu.CompilerParams(dimension_semantics=("parallel",),
                                             vmem_limit_bytes=IN_VMEM_BYTES),
        name="inproj",
    )(x2, g, w, w, w, wt)


def _retention_kernel(lg_ref, q_ref, k_ref, v_ref, gate_ref, g_ref, o_ref,
                      dmat_ref, kdec_ref, qdec_ref, sdec_ref, smask_ref, state_ref):
    c = RET_CHUNK
    n_pairs = HEADS // 2
    scale = DK ** -0.5

    @pl.when(pl.program_id(1) == 0)
    def _():
        n = lax.broadcasted_iota(jnp.int32, (c, c), 0)
        mm = lax.broadcasted_iota(jnp.int32, (c, c), 1)
        rel = (n - mm).astype(F32)
        causal = rel >= 0.0
        relc = jnp.maximum(rel, 0.0)
        pos_k = lax.broadcasted_iota(jnp.int32, (c, LANES), 0).astype(F32)
        lane_k = lax.broadcasted_iota(jnp.int32, (c, LANES), 1)
        pos_q = lax.broadcasted_iota(jnp.int32, (c, 2 * DV), 0).astype(F32)
        lane_q = lax.broadcasted_iota(jnp.int32, (c, 2 * DV), 1)
        srow = lax.broadcasted_iota(jnp.int32, (2 * DK, 2 * DV), 0)
        scol = lax.broadcasted_iota(jnp.int32, (2 * DK, 2 * DV), 1)
        in_a = (srow < DK) & (scol < DV)
        in_b = (srow >= DK) & (scol >= DV)
        smask_ref[...] = jnp.where(in_a | in_b, 1.0, 0.0)
        full_c = jnp.full((2 * DK, 2 * DV), c, F32)
        for pp in range(n_pairs):
            lga = lg_ref[2 * pp]
            lgb = lg_ref[2 * pp + 1]
            dmat_ref[pp, 0:c, :] = jnp.where(causal, jnp.exp(relc * lga), 0.0) * scale
            dmat_ref[pp, c:2 * c, :] = jnp.where(causal, jnp.exp(relc * lgb), 0.0) * scale
            kdec_ref[pp] = jnp.exp((c - 1.0 - pos_k) * jnp.where(lane_k < DK, lga, lgb)) * scale
            qdec_ref[pp] = jnp.exp((pos_q + 1.0) * jnp.where(lane_q < DV, lga, lgb))
            sdec_ref[pp] = (jnp.where(in_a, jnp.exp(full_c * lga), 0.0)
                            + jnp.where(in_b, jnp.exp(full_c * lgb), 0.0))
        state_ref[...] = jnp.zeros_like(state_ref)

    def chunk(ci, carry):
        rows = pl.ds(pl.multiple_of(ci * c, c), c)
        lane = lax.broadcasted_iota(jnp.int32, (c, LANES), 1)
        for pp in range(n_pairs):
            qk_cols = slice(pp * LANES, (pp + 1) * LANES)
            v_cols = slice(pp * 2 * DV, (pp + 1) * 2 * DV)
            qc = q_ref[rows, qk_cols]
            kc = k_ref[rows, qk_cols]
            vc = v_ref[rows, v_cols]
            q32 = qc.astype(F32)
            qa = jnp.where(lane < DK, q32, 0.0).astype(BF16)
            qb = jnp.where(lane >= DK, q32, 0.0).astype(BF16)
            qs = jnp.concatenate([qa, qb], axis=0)
            sc = lax.dot_general(qs, kc, _NT, preferred_element_type=F32)
            pm = (sc * dmat_ref[pp]).astype(BF16)
            intra_a = jnp.dot(pm[0:c], vc[:, 0:DV], preferred_element_type=F32)
            intra_b = jnp.dot(pm[c:2 * c], vc[:, DV:2 * DV], preferred_element_type=F32)
            st = state_ref[pp]
            inter = jnp.dot(qc, st.astype(BF16), preferred_element_type=F32)
            out = jnp.concatenate([intra_a, intra_b], axis=1) + inter * qdec_ref[pp]

            kd_t = (kc.astype(F32) * kdec_ref[pp]).T.astype(BF16)
            upd = jnp.dot(kd_t, vc, preferred_element_type=F32)
            state_ref[pp] = st * sdec_ref[pp] + upd * smask_ref[...]

            g = g_ref[:, v_cols]
            act = gate_ref[rows, v_cols].astype(F32)
            halves = []
            for hh in range(2):
                o = out[:, hh * DV:(hh + 1) * DV]
                ms = jnp.mean(o * o, axis=-1, keepdims=True)
                halves.append(o * lax.rsqrt(ms + EPS) * g[:, hh * DV:(hh + 1) * DV])
            o_ref[rows, v_cols] = (jnp.concatenate(halves, axis=1) * act).astype(o_ref.dtype)
        return carry

    lax.fori_loop(0, RET_ROWS // c, chunk, 0)


def _retention(z3, log_gamma, g):
    b, seq, _ = z3.shape
    c = RET_CHUNK
    n_pairs = HEADS // 2
    qk_w = n_pairs * LANES
    v_w = HEADS * DV
    return pl.pallas_call(
        _retention_kernel,
        grid=(b, seq // RET_ROWS),
        in_specs=[
            pl.BlockSpec(memory_space=pltpu.SMEM),
            pl.BlockSpec((None, RET_ROWS, qk_w), lambda i, s: (i, s, COL_QR * LANES // qk_w)),
            pl.BlockSpec((None, RET_ROWS, qk_w), lambda i, s: (i, s, COL_KR * LANES // qk_w)),
            pl.BlockSpec((None, RET_ROWS, v_w), lambda i, s: (i, s, COL_VR * LANES // v_w)),
            pl.BlockSpec((None, RET_ROWS, v_w), lambda i, s: (i, s, COL_GR * LANES // v_w)),
            pl.BlockSpec((1, v_w), lambda i, s: (0, 0)),
        ],
        out_specs=pl.BlockSpec((None, RET_ROWS, v_w), lambda i, s: (i, s, 0)),
        out_shape=jax.ShapeDtypeStruct((b, seq, v_w), BF16),
        scratch_shapes=[
            pltpu.VMEM((n_pairs, 2 * c, c), F32),
            pltpu.VMEM((n_pairs, c, LANES), F32),
            pltpu.VMEM((n_pairs, c, 2 * DV), F32),
            pltpu.VMEM((n_pairs, 2 * DK, 2 * DV), F32),
            pltpu.VMEM((2 * DK, 2 * DV), F32),
            pltpu.VMEM((n_pairs, 2 * DK, 2 * DV), F32),
        ],
        compiler_params=pltpu.CompilerParams(dimension_semantics=("parallel", "arbitrary")),
        name="retention",
    )(log_gamma, z3, z3, z3, z3, g)


def _diffattn_kernel(slope_ref, qt_ref, k_ref, vt_ref, gate_ref, gq_ref, gk_ref,
                     lq1_ref, lk1_ref, lq2_ref, lk2_ref, gs_ref, o_ref,
                     kn_ref, qs_ref, sa_ref, sb_ref, m_ref, acc_ref, *, lam_init):
    tq, tk = ATT_TQ, ATT_TK
    seq = k_ref.shape[0]
    sub = tq // tk
    slope = slope_ref[pl.program_id(1)]
    lam = (jnp.exp(jnp.sum(lq1_ref[...] * lk1_ref[...], axis=-1, keepdims=True))
           - jnp.exp(jnp.sum(lq2_ref[...] * lk2_ref[...], axis=-1, keepdims=True))
           + lam_init)
    same_map = ((lax.broadcasted_iota(jnp.int32, (LANES, LANES), 0) < DK)
                == (lax.broadcasted_iota(jnp.int32, (LANES, LANES), 1) < DK)).astype(F32).astype(BF16)

    def knorm(r, carry):
        rows = pl.ds(pl.multiple_of(r * NORM_ROWS, NORM_ROWS), NORM_ROWS)
        k32 = k_ref[rows, :].astype(F32)
        sq = k32 * k32
        hi = sq.astype(BF16)
        lo = (sq - hi.astype(F32)).astype(BF16)
        ss = (jnp.dot(hi, same_map, preferred_element_type=F32)
              + jnp.dot(lo, same_map, preferred_element_type=F32))
        kn_ref[rows, 0:LANES] = (k32 * lax.rsqrt(ss * (1.0 / DK) + EPS) * gk_ref[...]).astype(BF16)
        pos = r * NORM_ROWS + lax.broadcasted_iota(jnp.int32, (NORM_ROWS, LANES), 0)
        lane = lax.broadcasted_iota(jnp.int32, (NORM_ROWS, LANES), 1)
        lo = pos & (tk - 1)
        part = jnp.where(lane >= 2 * POS_COPIES, 0, jnp.where((lane & 1) == 0, lo, pos - lo))
        kn_ref[rows, LANES:2 * LANES] = part.astype(F32).astype(BF16)
        return carry
    lax.fori_loop(0, seq // NORM_ROWS, knorm, 0, unroll=4)

    c_full = jnp.full((LANES, 2 * tq), slope * LOG2E, F32)
    c0 = c_full.astype(BF16).astype(F32)
    c1 = (c_full - c0).astype(BF16).astype(F32)
    c2 = c_full - c0 - c1
    piece = lax.broadcasted_iota(jnp.int32, (LANES, 2 * tq), 0) >> 1
    qs_ref[LANES:2 * LANES, :] = jnp.where(
        piece == 0, c0, jnp.where(piece == 1, c1, jnp.where(piece == 2, c2, 0.0))).astype(BF16)
    ones_rows = jnp.ones((SUM_ROWS, tk), BF16)

    def q_tile(qi, carry):
        qt = jnp.concatenate([qt_ref[sub * qi + c] for c in range(sub)], axis=1).astype(F32)
        gq = gq_ref[...]
        halves = []
        for hh in range(2):
            x = qt[hh * DK:(hh + 1) * DK]
            ms = jnp.mean(x * x, axis=0, keepdims=True)
            halves.append(x * lax.rsqrt(ms + EPS) * gq[hh * DK:(hh + 1) * DK] * (DK ** -0.5 * LOG2E))
        zeros = jnp.zeros((DK, tk), F32)
        q1, q2 = halves
        qs_ref[0:DK, :] = jnp.concatenate(
            [piece for c in range(sub) for piece in (q1[:, c * tk:(c + 1) * tk], zeros)], axis=1).astype(BF16)
        qs_ref[DK:2 * DK, :] = jnp.concatenate(
            [piece for c in range(sub) for piece in (zeros, q2[:, c * tk:(c + 1) * tk])], axis=1).astype(BF16)
        m_ref[...] = jnp.full_like(m_ref, NEG)
        acc_ref[...] = jnp.zeros_like(acc_ref)
        n_lanes = 2 * tq
        bufs = (sa_ref, sb_ref)

        def scores(j, dst, start=0):
            keys = pl.ds(pl.multiple_of(j * tk, tk), tk)
            dst[:, start:n_lanes] = jnp.dot(kn_ref[keys, :], qs_ref[:, start:n_lanes],
                                            preferred_element_type=F32)

        def absorb(j, src, start=0, diag=False):
            lanes = slice(start, n_lanes)
            s = src[:, lanes]
            if diag:
                key = lax.broadcasted_iota(jnp.int32, (tk, 2 * tk), 0)
                qry = lax.broadcasted_iota(jnp.int32, (tk, 2 * tk), 1) & (tk - 1)
                masked = jnp.where(key <= qry, src[:, start:start + 2 * tk], NEG)
                rest = [src[:, start + 2 * tk:n_lanes]] if start + 2 * tk < n_lanes else []
                s = jnp.concatenate([masked] + rest, axis=1)
            m_old = m_ref[:, lanes]
            m_new = jnp.maximum(m_old, jnp.max(s, axis=0, keepdims=True))
            alpha = jnp.exp2(m_old - m_new)
            p = jnp.exp2(s - m_new).astype(BF16)
            v_aug = jnp.concatenate([vt_ref[j], ones_rows], axis=0)
            acc_ref[:, lanes] = alpha * acc_ref[:, lanes] + jnp.dot(v_aug, p, preferred_element_type=F32)
            m_ref[:, lanes] = m_new

        scores(0, sa_ref)

        def quad(mi, c2):
            for u in range(4):
                scores(4 * mi + u + 1, bufs[(u + 1) % 2])
                absorb(4 * mi + u, bufs[u % 2])
            return c2
        lax.fori_loop(0, qi * (sub // 4), quad, 0)

        first_diag = sub * qi
        for d in range(sub):
            if d + 1 < sub:
                scores(first_diag + d + 1, bufs[(d + 1) % 2], start=(d + 1) * 2 * tk)
            absorb(first_diag + d, bufs[d % 2], start=d * 2 * tk, diag=True)

        inv = 1.0 / acc_ref[DV:DV + 1, :]
        acc = acc_ref[0:DV, :] * inv
        o1 = jnp.concatenate([acc[:, 2 * c * tk:(2 * c + 1) * tk] for c in range(sub)], axis=1)
        o2 = jnp.concatenate([acc[:, (2 * c + 1) * tk:(2 * c + 2) * tk] for c in range(sub)], axis=1)
        ot = o1 - lam * o2
        ms = jnp.mean(ot * ot, axis=0, keepdims=True)
        on = (ot * lax.rsqrt(ms + EPS)).T
        rows = pl.ds(pl.multiple_of(qi * tq, tq), tq)
        act = gate_ref[rows, :].astype(F32)
        o_ref[rows, :] = (on * gs_ref[...] * (1.0 - lam_init) * act).astype(o_ref.dtype)
        return carry

    lax.fori_loop(0, seq // tq, q_tile, 0)


def _diffattn(z3, t4, slopes, gq, gk, lq1, lk1, lq2, lk2, gs, lam_init):
    b, seq, _ = z3.shape
    tq, tk = ATT_TQ, ATT_TK
    assert tq == 4 * tk
    head_rows = lambda col: pl.BlockSpec((None, seq, LANES), lambda i, h: (i, 0, col + h))
    head_t = lambda off: pl.BlockSpec((None, seq // tk, LANES, tk), lambda i, h: (i, 0, off + h, 0))
    vec = lambda width: pl.BlockSpec((1, width), lambda i, h: (0, 0))
    return pl.pallas_call(
        functools.partial(_diffattn_kernel, lam_init=lam_init),
        grid=(b, HEADS),
        in_specs=[
            pl.BlockSpec(memory_space=pltpu.SMEM),
            head_t(0), head_rows(COL_KD), head_t(HEADS), head_rows(COL_GD),
            pl.BlockSpec((LANES, 1), lambda i, h: (0, 0)),
            vec(LANES), vec(DK), vec(DK), vec(DK), vec(DK),
            pl.BlockSpec((1, DV), lambda i, h: (0, h)),
        ],
        out_specs=pl.BlockSpec((None, seq, DV), lambda i, h: (i, 0, h)),
        out_shape=jax.ShapeDtypeStruct((b, seq, HEADS * DV), BF16),
        scratch_shapes=[
            pltpu.VMEM((seq, 2 * LANES), BF16),
            pltpu.VMEM((2 * LANES, 2 * tq), BF16),
            pltpu.VMEM((tk, 2 * tq), F32),
            pltpu.VMEM((tk, 2 * tq), F32),
            pltpu.VMEM((1, 2 * tq), F32),
            pltpu.VMEM((DV + SUM_ROWS, 2 * tq), F32),
        ],
        compiler_params=pltpu.CompilerParams(dimension_semantics=("parallel", "parallel")),
        name="diffattn",
    )(slopes, t4, z3, t4, z3, gq, gk, lq1, lk1, lq2, lk2, gs)


def _outproj_kernel(x_ref, or_ref, od_ref, mr_ref, md_ref, wr_ref, wd_ref, wo_ref, out_ref):
    y_r = jnp.dot(or_ref[...], wr_ref[...], preferred_element_type=F32)
    y_d = jnp.dot(od_ref[...], wd_ref[...], preferred_element_type=F32)
    merged = mr_ref[...].astype(F32) * y_r + md_ref[...].astype(F32) * y_d
    out_ref[...] = x_ref[...] + jnp.dot(merged.astype(BF16), wo_ref[...], preferred_element_type=F32)


def _outproj(x2, o_r, o_d, z2, w_r, w_d, w_o):
    m = x2.shape[0]
    rows = lambda col: pl.BlockSpec((OUT_TM, D_MODEL), lambda i: (i, col))
    weight = pl.BlockSpec((D_MODEL, D_MODEL), lambda i: (0, 0))
    return pl.pallas_call(
        _outproj_kernel,
        grid=(m // OUT_TM,),
        in_specs=[rows(0), rows(0), rows(0),
                  rows(COL_MR * LANES // D_MODEL), rows(COL_MD * LANES // D_MODEL),
                  weight, weight, weight],
        out_specs=rows(0),
        out_shape=jax.ShapeDtypeStruct((m, D_MODEL), F32),
        compiler_params=pltpu.CompilerParams(dimension_semantics=("parallel",)),
        name="outproj",
    )(x2, o_r, o_d, z2, z2, w_r, w_d, w_o)


def kernel(x, norm_g, w_in, ret_norm_g, ret_w_o, diff_q_norm_g, diff_k_norm_g,
           diff_lq1, diff_lk1, diff_lq2, diff_lk2, diff_sub_norm_g, diff_w_o, w_out):
    b, seq, d = x.shape
    depth = norm_g.shape[0]
    assert d == D_MODEL and w_in.shape[-1] == N_IN
    assert seq % ATT_TQ == 0 and seq % RET_ROWS == 0 and RET_ROWS % RET_CHUNK == 0 and seq % IN_TM == 0

    assert seq <= ATT_TK * 256
    slopes = jnp.exp2(-8.0 * jnp.arange(1, HEADS + 1, dtype=F32) / HEADS)
    log_gamma = jnp.log1p(-jnp.exp2(-5.0 - jnp.arange(HEADS, dtype=F32)))
    row = lambda v: v.reshape(1, -1).astype(F32)

    x2 = x.reshape(b * seq, d)
    for l in range(depth):
        lam_init = 0.8 - 0.6 * math.exp(-0.3 * l)
        w = w_in[l]
        wb = w.astype(BF16)
        w_t = jnp.concatenate([wb[:, 3072:4096], wb[:, 5120:6144]], axis=1).T
        z2, t4 = _inproj(x2, row(norm_g[l]), wb, w_t, b, seq)
        z3 = z2.reshape(b, seq, N_MAIN)
        o_r = _retention(z3, log_gamma, row(ret_norm_g[l]))
        gq_col = jnp.concatenate([diff_q_norm_g[l], diff_q_norm_g[l]]).reshape(-1, 1).astype(F32)
        gk_row = jnp.concatenate([diff_k_norm_g[l], diff_k_norm_g[l]]).reshape(1, -1).astype(F32)
        o_d = _diffattn(z3, t4, slopes, gq_col, gk_row,
                        row(diff_lq1[l]), row(diff_lk1[l]), row(diff_lq2[l]), row(diff_lk2[l]),
                        row(diff_sub_norm_g[l]), lam_init)
        x2 = _outproj(x2, o_r.reshape(b * seq, -1), o_d.reshape(b * seq, -1), z2,
                      ret_w_o[l].astype(BF16), diff_w_o[l].astype(BF16), w_out[l].astype(BF16))
    return x2.reshape(b, seq, d)
```

```python
import functools
import math

import jax
import jax.numpy as jnp
from jax import lax
from jax.experimental import pallas as pl
from jax.experimental.pallas import tpu as pltpu

D_MODEL = 1024
HEADS = 8
DK = 64
DV = 128
EPS = 1e-6
LANES = 128
N_IN = 9216
N_MAIN = 7168
N_T = 2048

COL_QR, COL_KR, COL_VR, COL_GR = 0, 4, 8, 16
COL_KD, COL_GD, COL_MR, COL_MD = 24, 32, 40, 48

IN_TM, IN_TN = 512, 1024
IN_VMEM_BYTES = (2 * D_MODEL * (N_MAIN + N_T) + 2 * IN_TM * D_MODEL * 4 + 2 * IN_TM * (N_MAIN + N_T) * 2
                 + IN_TM * D_MODEL * 2 + 2 * IN_TM * IN_TN * 4 + (1 << 20))
NORM_ROWS = 256
RET_CHUNK = 256
RET_ROWS = 1024
ATT_TQ = 1024
ATT_TK = 256
OUT_TM = 512
NEG = -1e30
LOG2E = 1.4426950408889634
POS_COPIES = 3
SUM_ROWS = 16

F32 = jnp.float32
BF16 = jnp.bfloat16
_NT = (((1,), (1,)), ((), ()))


def _silu(v):
    return v * jax.nn.sigmoid(v)


_CHUNK_PLAN = (
    ("z", None),
    ("z", None),
    ("z", _silu),
    ("t", 0),
    ("z", None),
    ("t", 1),
    ("z", _silu),
    ("z", jax.nn.sigmoid),
    ("z", jax.nn.sigmoid),
)


def _inproj_kernel(x_ref, g_ref, w_ref, z_ref, t_ref, h_ref):
    for r in range(IN_TM // NORM_ROWS):
        rows = slice(r * NORM_ROWS, (r + 1) * NORM_ROWS)
        x = x_ref[rows, :]
        ms = jnp.mean(x * x, axis=-1, keepdims=True)
        h_ref[rows, :] = (x * lax.rsqrt(ms + EPS) * g_ref[...]).astype(BF16)
    h = h_ref[...]
    z_chunk = 0
    for n, (kind, arg) in enumerate(_CHUNK_PLAN):
        y = jnp.dot(h, w_ref[:, n * IN_TN:(n + 1) * IN_TN], preferred_element_type=F32)
        if kind == "t":
            yt = y.T.astype(t_ref.dtype)
            for c in range(IN_TM // ATT_TK):
                t_ref[c, arg * IN_TN:(arg + 1) * IN_TN, :] = yt[:, c * ATT_TK:(c + 1) * ATT_TK]
        else:
            if arg is not None:
                y = arg(y)
            z_ref[:, z_chunk * IN_TN:(z_chunk + 1) * IN_TN] = y.astype(z_ref.dtype)
            z_chunk += 1


def _inproj(x2, g, w_all, layer, batch, seq):
    m = x2.shape[0]
    per_batch = seq // IN_TM
    sub = IN_TM // ATT_TK
    resident = dict(pipeline_mode=pl.Buffered(1))
    return pl.pallas_call(
        _inproj_kernel,
        grid=(m // IN_TM,),
        in_specs=[
            pl.BlockSpec((IN_TM, D_MODEL), lambda i: (i, 0)),
            pl.BlockSpec((1, D_MODEL), lambda i: (0, 0)),
            pl.BlockSpec((None, D_MODEL, N_IN), lambda i: (layer, 0, 0), **resident),
        ],
        out_specs=[
            pl.BlockSpec((IN_TM, N_MAIN), lambda i: (i, 0)),
            pl.BlockSpec((None, sub, N_T, ATT_TK), lambda i: (i // per_batch, i % per_batch, 0, 0)),
        ],
        out_shape=[
            jax.ShapeDtypeStruct((m, N_MAIN), BF16),
            jax.ShapeDtypeStruct((batch, seq // ATT_TK, N_T, ATT_TK), BF16),
        ],
        scratch_shapes=[pltpu.VMEM((IN_TM, D_MODEL), BF16)],
        compiler_params=pltpu.CompilerParams(dimension_semantics=("parallel",),
                                             vmem_limit_bytes=IN_VMEM_BYTES),
        name="inproj",
    )(x2, g, w_all)


def _retention_kernel(lg_ref, q_ref, k_ref, v_ref, gate_ref, g_ref, o_ref,
                      dmat_ref, kdec_ref, qdec_ref, sdec_ref, smask_ref, state_ref):
    c = RET_CHUNK
    n_pairs = HEADS // 2
    scale = DK ** -0.5

    @pl.when(pl.program_id(1) == 0)
    def _():
        n = lax.broadcasted_iota(jnp.int32, (c, c), 0)
        mm = lax.broadcasted_iota(jnp.int32, (c, c), 1)
        rel = (n - mm).astype(F32)
        causal = rel >= 0.0
        relc = jnp.maximum(rel, 0.0)
        pos_k = lax.broadcasted_iota(jnp.int32, (c, LANES), 0).astype(F32)
        lane_k = lax.broadcasted_iota(jnp.int32, (c, LANES), 1)
        pos_q = lax.broadcasted_iota(jnp.int32, (c, 2 * DV), 0).astype(F32)
        lane_q = lax.broadcasted_iota(jnp.int32, (c, 2 * DV), 1)
        srow = lax.broadcasted_iota(jnp.int32, (2 * DK, 2 * DV), 0)
        scol = lax.broadcasted_iota(jnp.int32, (2 * DK, 2 * DV), 1)
        in_a = (srow < DK) & (scol < DV)
        in_b = (srow >= DK) & (scol >= DV)
        smask_ref[...] = jnp.where(in_a | in_b, 1.0, 0.0)
        full_c = jnp.full((2 * DK, 2 * DV), c, F32)
        for pp in range(n_pairs):
            lga = lg_ref[2 * pp]
            lgb = lg_ref[2 * pp + 1]
            dmat_ref[pp, 0:c, :] = jnp.where(causal, jnp.exp(relc * lga), 0.0) * scale
            dmat_ref[pp, c:2 * c, :] = jnp.where(causal, jnp.exp(relc * lgb), 0.0) * scale
            kdec_ref[pp] = jnp.exp((c - 1.0 - pos_k) * jnp.where(lane_k < DK, lga, lgb)) * scale
            qdec_ref[pp] = jnp.exp((pos_q + 1.0) * jnp.where(lane_q < DV, lga, lgb))
            sdec_ref[pp] = (jnp.where(in_a, jnp.exp(full_c * lga), 0.0)
                            + jnp.where(in_b, jnp.exp(full_c * lgb), 0.0))
        state_ref[...] = jnp.zeros_like(state_ref)

    def chunk(ci, carry):
        rows = pl.ds(pl.multiple_of(ci * c, c), c)
        lane = lax.broadcasted_iota(jnp.int32, (c, LANES), 1)
        for pp in range(n_pairs):
            qk_cols = slice(pp * LANES, (pp + 1) * LANES)
            v_cols = slice(pp * 2 * DV, (pp + 1) * 2 * DV)
            qc = q_ref[rows, qk_cols]
            kc = k_ref[rows, qk_cols]
            vc = v_ref[rows, v_cols]
            q32 = qc.astype(F32)
            qa = jnp.where(lane < DK, q32, 0.0).astype(BF16)
            qb = jnp.where(lane >= DK, q32, 0.0).astype(BF16)
            qs = jnp.concatenate([qa, qb], axis=0)
            sc = lax.dot_general(qs, kc, _NT, preferred_element_type=F32)
            pm = (sc * dmat_ref[pp]).astype(BF16)
            intra_a = jnp.dot(pm[0:c], vc[:, 0:DV], preferred_element_type=F32)
            intra_b = jnp.dot(pm[c:2 * c], vc[:, DV:2 * DV], preferred_element_type=F32)
            st = state_ref[pp]
            inter = jnp.dot(qc, st.astype(BF16), preferred_element_type=F32)
            out = jnp.concatenate([intra_a, intra_b], axis=1) + inter * qdec_ref[pp]

            kd_t = (kc.astype(F32) * kdec_ref[pp]).T.astype(BF16)
            upd = jnp.dot(kd_t, vc, preferred_element_type=F32)
            state_ref[pp] = st * sdec_ref[pp] + upd * smask_ref[...]

            g = g_ref[:, v_cols]
            act = gate_ref[rows, v_cols].astype(F32)
            halves = []
            for hh in range(2):
                o = out[:, hh * DV:(hh + 1) * DV]
                ms = jnp.mean(o * o, axis=-1, keepdims=True)
                halves.append(o * lax.rsqrt(ms + EPS) * g[:, hh * DV:(hh + 1) * DV])
            o_ref[rows, v_cols] = (jnp.concatenate(halves, axis=1) * act).astype(o_ref.dtype)
        return carry

    lax.fori_loop(0, RET_ROWS // c, chunk, 0)


def _retention(z3, log_gamma, g):
    b, seq, _ = z3.shape
    c = RET_CHUNK
    n_pairs = HEADS // 2
    qk_w = n_pairs * LANES
    v_w = HEADS * DV
    return pl.pallas_call(
        _retention_kernel,
        grid=(b, seq // RET_ROWS),
        in_specs=[
            pl.BlockSpec(memory_space=pltpu.SMEM),
            pl.BlockSpec((None, RET_ROWS, qk_w), lambda i, s: (i, s, COL_QR * LANES // qk_w)),
            pl.BlockSpec((None, RET_ROWS, qk_w), lambda i, s: (i, s, COL_KR * LANES // qk_w)),
            pl.BlockSpec((None, RET_ROWS, v_w), lambda i, s: (i, s, COL_VR * LANES // v_w)),
            pl.BlockSpec((None, RET_ROWS, v_w), lambda i, s: (i, s, COL_GR * LANES // v_w)),
            pl.BlockSpec((1, v_w), lambda i, s: (0, 0)),
        ],
        out_specs=pl.BlockSpec((None, RET_ROWS, v_w), lambda i, s: (i, s, 0)),
        out_shape=jax.ShapeDtypeStruct((b, seq, v_w), BF16),
        scratch_shapes=[
            pltpu.VMEM((n_pairs, 2 * c, c), F32),
            pltpu.VMEM((n_pairs, c, LANES), F32),
            pltpu.VMEM((n_pairs, c, 2 * DV), F32),
            pltpu.VMEM((n_pairs, 2 * DK, 2 * DV), F32),
            pltpu.VMEM((2 * DK, 2 * DV), F32),
            pltpu.VMEM((n_pairs, 2 * DK, 2 * DV), F32),
        ],
        compiler_params=pltpu.CompilerParams(dimension_semantics=("parallel", "arbitrary")),
        name="retention",
    )(log_gamma, z3, z3, z3, z3, g)


def _diffattn_kernel(slope_ref, qt_ref, k_ref, vt_ref, gate_ref, gq_ref, gk_ref,
                     lq1_ref, lk1_ref, lq2_ref, lk2_ref, gs_ref, o_ref,
                     kn_ref, qs_ref, sa_ref, sb_ref, mx_ref, m_ref, acc_ref, *, lam_init):
    tq, tk = ATT_TQ, ATT_TK
    seq = k_ref.shape[0]
    sub = tq // tk
    slope = slope_ref[pl.program_id(1)]
    lam = (jnp.exp(jnp.sum(lq1_ref[...] * lk1_ref[...], axis=-1, keepdims=True))
           - jnp.exp(jnp.sum(lq2_ref[...] * lk2_ref[...], axis=-1, keepdims=True))
           + lam_init)
    same_map = ((lax.broadcasted_iota(jnp.int32, (LANES, LANES), 0) < DK)
                == (lax.broadcasted_iota(jnp.int32, (LANES, LANES), 1) < DK)).astype(F32).astype(BF16)

    def knorm(r, carry):
        rows = pl.ds(pl.multiple_of(r * NORM_ROWS, NORM_ROWS), NORM_ROWS)
        k32 = k_ref[rows, :].astype(F32)
        sq = k32 * k32
        hi = sq.astype(BF16)
        lo = (sq - hi.astype(F32)).astype(BF16)
        ss = (jnp.dot(hi, same_map, preferred_element_type=F32)
              + jnp.dot(lo, same_map, preferred_element_type=F32))
        kn_ref[rows, 0:LANES] = (k32 * lax.rsqrt(ss * (1.0 / DK) + EPS) * gk_ref[...]).astype(BF16)
        pos = r * NORM_ROWS + lax.broadcasted_iota(jnp.int32, (NORM_ROWS, LANES), 0)
        lane = lax.broadcasted_iota(jnp.int32, (NORM_ROWS, LANES), 1)
        lo = pos & (tk - 1)
        part = jnp.where(lane >= 2 * POS_COPIES, 0, jnp.where((lane & 1) == 0, lo, pos - lo))
        kn_ref[rows, LANES:2 * LANES] = part.astype(F32).astype(BF16)
        return carry
    lax.fori_loop(0, seq // NORM_ROWS, knorm, 0, unroll=4)

    c_full = jnp.full((LANES, 2 * tq), slope * LOG2E, F32)
    c0 = c_full.astype(BF16).astype(F32)
    c1 = (c_full - c0).astype(BF16).astype(F32)
    c2 = c_full - c0 - c1
    piece = lax.broadcasted_iota(jnp.int32, (LANES, 2 * tq), 0) >> 1
    qs_ref[LANES:2 * LANES, :] = jnp.where(
        piece == 0, c0, jnp.where(piece == 1, c1, jnp.where(piece == 2, c2, 0.0))).astype(BF16)
    ones_rows = jnp.ones((SUM_ROWS, tk), BF16)

    def q_tile(qi, carry):
        qt = jnp.concatenate([qt_ref[sub * qi + c] for c in range(sub)], axis=1).astype(F32)
        gq = gq_ref[...]
        halves = []
        for hh in range(2):
            x = qt[hh * DK:(hh + 1) * DK]
            ms = jnp.mean(x * x, axis=0, keepdims=True)
            halves.append(x * lax.rsqrt(ms + EPS) * gq[hh * DK:(hh + 1) * DK] * (DK ** -0.5 * LOG2E))
        zeros = jnp.zeros((DK, tk), F32)
        q1, q2 = halves
        qs_ref[0:DK, :] = jnp.concatenate(
            [piece for c in range(sub) for piece in (q1[:, c * tk:(c + 1) * tk], zeros)], axis=1).astype(BF16)
        qs_ref[DK:2 * DK, :] = jnp.concatenate(
            [piece for c in range(sub) for piece in (zeros, q2[:, c * tk:(c + 1) * tk])], axis=1).astype(BF16)
        m_ref[...] = jnp.full_like(m_ref, NEG)
        acc_ref[...] = jnp.zeros_like(acc_ref)
        n_lanes = 2 * tq
        bufs = (sa_ref, sb_ref)

        def scores(j, b, start=0, with_max=True):
            keys = pl.ds(pl.multiple_of(j * tk, tk), tk)
            sv = jnp.dot(kn_ref[keys, :], qs_ref[:, start:n_lanes], preferred_element_type=F32)
            bufs[b][:, start:n_lanes] = sv
            if with_max:
                mx_ref[b] = jnp.max(sv, axis=0, keepdims=True)

        def absorb(j, b, start=0, diag=False):
            src = bufs[b]
            lanes = slice(start, n_lanes)
            s = src[:, lanes]
            if diag:
                key = lax.broadcasted_iota(jnp.int32, (tk, 2 * tk), 0)
                qry = lax.broadcasted_iota(jnp.int32, (tk, 2 * tk), 1) & (tk - 1)
                masked = jnp.where(key <= qry, src[:, start:start + 2 * tk], NEG)
                rest = [src[:, start + 2 * tk:n_lanes]] if start + 2 * tk < n_lanes else []
                s = jnp.concatenate([masked] + rest, axis=1)
                m_tile = jnp.max(s, axis=0, keepdims=True)
            else:
                m_tile = mx_ref[b]
            m_old = m_ref[:, lanes]
            m_new = jnp.maximum(m_old, m_tile)
            alpha = jnp.exp2(m_old - m_new)
            p = jnp.exp2(s - m_new).astype(BF16)
            v_aug = jnp.concatenate([vt_ref[j], ones_rows], axis=0)
            acc_ref[:, lanes] = alpha * acc_ref[:, lanes] + jnp.dot(v_aug, p, preferred_element_type=F32)
            m_ref[:, lanes] = m_new

        scores(0, 0)

        def quad(mi, c2):
            for u in range(4):
                scores(4 * mi + u + 1, (u + 1) % 2)
                absorb(4 * mi + u, u % 2)
            return c2
        lax.fori_loop(0, qi * (sub // 4), quad, 0)

        first_diag = sub * qi
        for d in range(sub):
            if d + 1 < sub:
                scores(first_diag + d + 1, (d + 1) % 2, start=(d + 1) * 2 * tk, with_max=False)
            absorb(first_diag + d, d % 2, start=d * 2 * tk, diag=True)

        inv = 1.0 / acc_ref[DV:DV + 1, :]
        acc = acc_ref[0:DV, :] * inv
        o1 = jnp.concatenate([acc[:, 2 * c * tk:(2 * c + 1) * tk] for c in range(sub)], axis=1)
        o2 = jnp.concatenate([acc[:, (2 * c + 1) * tk:(2 * c + 2) * tk] for c in range(sub)], axis=1)
        ot = o1 - lam * o2
        ms = jnp.mean(ot * ot, axis=0, keepdims=True)
        on = (ot * lax.rsqrt(ms + EPS)).T
        rows = pl.ds(pl.multiple_of(qi * tq, tq), tq)
        act = gate_ref[rows, :].astype(F32)
        o_ref[rows, :] = (on * gs_ref[...] * (1.0 - lam_init) * act).astype(o_ref.dtype)
        return carry

    lax.fori_loop(0, seq // tq, q_tile, 0)


def _diffattn(z3, t4, slopes, gq, gk, lq1, lk1, lq2, lk2, gs, lam_init):
    b, seq, _ = z3.shape
    tq, tk = ATT_TQ, ATT_TK
    assert tq == 4 * tk
    head_rows = lambda col: pl.BlockSpec((None, seq, LANES), lambda i, h: (i, 0, col + h))
    head_t = lambda off: pl.BlockSpec((None, seq // tk, LANES, tk), lambda i, h: (i, 0, off + h, 0))
    vec = lambda width: pl.BlockSpec((1, width), lambda i, h: (0, 0))
    return pl.pallas_call(
        functools.partial(_diffattn_kernel, lam_init=lam_init),
        grid=(b, HEADS),
        in_specs=[
            pl.BlockSpec(memory_space=pltpu.SMEM),
            head_t(0), head_rows(COL_KD), head_t(HEADS), head_rows(COL_GD),
            pl.BlockSpec((LANES, 1), lambda i, h: (0, 0)),
            vec(LANES), vec(DK), vec(DK), vec(DK), vec(DK),
            pl.BlockSpec((1, DV), lambda i, h: (0, h)),
        ],
        out_specs=pl.BlockSpec((None, seq, DV), lambda i, h: (i, 0, h)),
        out_shape=jax.ShapeDtypeStruct((b, seq, HEADS * DV), BF16),
        scratch_shapes=[
            pltpu.VMEM((seq, 2 * LANES), BF16),
            pltpu.VMEM((2 * LANES, 2 * tq), BF16),
            pltpu.VMEM((tk, 2 * tq), F32),
            pltpu.VMEM((tk, 2 * tq), F32),
            pltpu.VMEM((2, 1, 2 * tq), F32),
            pltpu.VMEM((1, 2 * tq), F32),
            pltpu.VMEM((DV + SUM_ROWS, 2 * tq), F32),
        ],
        compiler_params=pltpu.CompilerParams(dimension_semantics=("parallel", "parallel")),
        name="diffattn",
    )(slopes, t4, z3, t4, z3, gq, gk, lq1, lk1, lq2, lk2, gs)


def _outproj_kernel(x_ref, or_ref, od_ref, mr_ref, md_ref, wr_ref, wd_ref, wo_ref, out_ref):
    y_r = jnp.dot(or_ref[...], wr_ref[...], preferred_element_type=F32)
    y_d = jnp.dot(od_ref[...], wd_ref[...], preferred_element_type=F32)
    merged = mr_ref[...].astype(F32) * y_r + md_ref[...].astype(F32) * y_d
    out_ref[...] = x_ref[...] + jnp.dot(merged.astype(BF16), wo_ref[...], preferred_element_type=F32)


def _outproj(x2, o_r, o_d, z2, w_r, w_d, w_o, layer):
    m = x2.shape[0]
    rows = lambda col: pl.BlockSpec((OUT_TM, D_MODEL), lambda i: (i, col))
    weight = pl.BlockSpec((None, D_MODEL, D_MODEL), lambda i: (layer, 0, 0))
    return pl.pallas_call(
        _outproj_kernel,
        grid=(m // OUT_TM,),
        in_specs=[rows(0), rows(0), rows(0),
                  rows(COL_MR * LANES // D_MODEL), rows(COL_MD * LANES // D_MODEL),
                  weight, weight, weight],
        out_specs=rows(0),
        out_shape=jax.ShapeDtypeStruct((m, D_MODEL), F32),
        compiler_params=pltpu.CompilerParams(dimension_semantics=("parallel",)),
        name="outproj",
    )(x2, o_r, o_d, z2, z2, w_r, w_d, w_o)


def kernel(x, norm_g, w_in, ret_norm_g, ret_w_o, diff_q_norm_g, diff_k_norm_g,
           diff_lq1, diff_lk1, diff_lq2, diff_lk2, diff_sub_norm_g, diff_w_o, w_out):
    b, seq, d = x.shape
    depth = norm_g.shape[0]
    assert d == D_MODEL and w_in.shape[-1] == N_IN
    assert seq % ATT_TQ == 0 and seq % RET_ROWS == 0 and RET_ROWS % RET_CHUNK == 0 and seq % IN_TM == 0

    assert seq <= ATT_TK * 256
    slopes = jnp.exp2(-8.0 * jnp.arange(1, HEADS + 1, dtype=F32) / HEADS)
    log_gamma = jnp.log1p(-jnp.exp2(-5.0 - jnp.arange(HEADS, dtype=F32)))
    row = lambda v: v.reshape(1, -1).astype(F32)

    w_in_b, w_r_b, w_d_b, w_o_b = (w.astype(BF16) for w in (w_in, ret_w_o, diff_w_o, w_out))

    x2 = x.reshape(b * seq, d)
    for l in range(depth):
        lam_init = 0.8 - 0.6 * math.exp(-0.3 * l)
        z2, t4 = _inproj(x2, row(norm_g[l]), w_in_b, l, b, seq)
        z3 = z2.reshape(b, seq, N_MAIN)
        o_r = _retention(z3, log_gamma, row(ret_norm_g[l]))
        gq_col = jnp.concatenate([diff_q_norm_g[l], diff_q_norm_g[l]]).reshape(-1, 1).astype(F32)
        gk_row = jnp.concatenate([diff_k_norm_g[l], diff_k_norm_g[l]]).reshape(1, -1).astype(F32)
        o_d = _diffattn(z3, t4, slopes, gq_col, gk_row,
                        row(diff_lq1[l]), row(diff_lk1[l]), row(diff_lq2[l]), row(diff_lk2[l]),
                        row(diff_sub_norm_g[l]), lam_init)
        x2 = _outproj(x2, o_r.reshape(b * seq, -1), o_d.reshape(b * seq, -1), z2,
                      w_r_b, w_d_b, w_o_b, l)
    return x2.reshape(b, seq, d)
```

```python
import functools
import math

import jax
import jax.numpy as jnp
from jax import lax
from jax.experimental import pallas as pl
from jax.experimental.pallas import tpu as pltpu

D_MODEL = 1024
HEADS = 8
DK = 64
DV = 128
EPS = 1e-6
LANES = 128
N_IN = 9216
N_MAIN = 7168
N_T = 2048

COL_QR, COL_KR, COL_VR, COL_GR = 0, 4, 8, 16
COL_KD, COL_GD, COL_MR, COL_MD = 24, 32, 40, 48

IN_TM, IN_TN = 512, 1024
IN_VMEM_BYTES = (2 * D_MODEL * (N_MAIN + N_T) + 2 * IN_TM * D_MODEL * 4 + 2 * IN_TM * (N_MAIN + N_T) * 2
                 + IN_TM * D_MODEL * 2 + 2 * IN_TM * IN_TN * 4 + (1 << 20))
NORM_ROWS = 256
RET_CHUNK = 256
RET_ROWS = 1024
ATT_TQ = 1024
ATT_TK = 256
OUT_TM = 512
NEG = -1e30
LOG2E = 1.4426950408889634
POS_COPIES = 3
SUM_ROWS = 16

F32 = jnp.float32
BF16 = jnp.bfloat16
_NT = (((1,), (1,)), ((), ()))


def _silu(v):
    return v * jax.nn.sigmoid(v)


_CHUNK_PLAN = (
    ("z", None),
    ("z", None),
    ("z", _silu),
    ("t", 0),
    ("z", None),
    ("t", 1),
    ("z", _silu),
    ("z", jax.nn.sigmoid),
    ("z", jax.nn.sigmoid),
)


def _inproj_kernel(x_ref, g_ref, w_ref, z_ref, t_ref, h_ref):
    for r in range(IN_TM // NORM_ROWS):
        rows = slice(r * NORM_ROWS, (r + 1) * NORM_ROWS)
        x = x_ref[rows, :]
        ms = jnp.mean(x * x, axis=-1, keepdims=True)
        h_ref[rows, :] = (x * lax.rsqrt(ms + EPS) * g_ref[...]).astype(BF16)
    h = h_ref[...]
    z_chunk = 0
    for n, (kind, arg) in enumerate(_CHUNK_PLAN):
        y = jnp.dot(h, w_ref[:, n * IN_TN:(n + 1) * IN_TN], preferred_element_type=F32)
        if kind == "t":
            yt = y.T.astype(t_ref.dtype)
            for c in range(IN_TM // ATT_TK):
                t_ref[c, arg * IN_TN:(arg + 1) * IN_TN, :] = yt[:, c * ATT_TK:(c + 1) * ATT_TK]
        else:
            if arg is not None:
                y = arg(y)
            z_ref[:, z_chunk * IN_TN:(z_chunk + 1) * IN_TN] = y.astype(z_ref.dtype)
            z_chunk += 1


def _inproj(x2, g, w_all, layer, batch, seq):
    m = x2.shape[0]
    per_batch = seq // IN_TM
    sub = IN_TM // ATT_TK
    resident = dict(pipeline_mode=pl.Buffered(1))
    return pl.pallas_call(
        _inproj_kernel,
        grid=(m // IN_TM,),
        in_specs=[
            pl.BlockSpec((IN_TM, D_MODEL), lambda i: (i, 0)),
            pl.BlockSpec((1, D_MODEL), lambda i: (0, 0)),
            pl.BlockSpec((None, D_MODEL, N_IN), lambda i: (layer, 0, 0), **resident),
        ],
        out_specs=[
            pl.BlockSpec((IN_TM, N_MAIN), lambda i: (i, 0)),
            pl.BlockSpec((None, sub, N_T, ATT_TK), lambda i: (i // per_batch, i % per_batch, 0, 0)),
        ],
        out_shape=[
            jax.ShapeDtypeStruct((m, N_MAIN), BF16),
            jax.ShapeDtypeStruct((batch, seq // ATT_TK, N_T, ATT_TK), BF16),
        ],
        scratch_shapes=[pltpu.VMEM((IN_TM, D_MODEL), BF16)],
        compiler_params=pltpu.CompilerParams(dimension_semantics=("parallel",),
                                             vmem_limit_bytes=IN_VMEM_BYTES),
        name="inproj",
    )(x2, g, w_all)


def _retention_kernel(lg_ref, q_ref, k_ref, v_ref, gate_ref, g_ref, o_ref,
                      dmat_ref, kdec_ref, qdec_ref, sdec_ref, smask_ref, state_ref):
    c = RET_CHUNK
    n_pairs = HEADS // 2
    scale = DK ** -0.5

    @pl.when(pl.program_id(1) == 0)
    def _():
        n = lax.broadcasted_iota(jnp.int32, (c, c), 0)
        mm = lax.broadcasted_iota(jnp.int32, (c, c), 1)
        rel = (n - mm).astype(F32)
        causal = rel >= 0.0
        relc = jnp.maximum(rel, 0.0)
        pos_k = lax.broadcasted_iota(jnp.int32, (c, LANES), 0).astype(F32)
        lane_k = lax.broadcasted_iota(jnp.int32, (c, LANES), 1)
        pos_q = lax.broadcasted_iota(jnp.int32, (c, 2 * DV), 0).astype(F32)
        lane_q = lax.broadcasted_iota(jnp.int32, (c, 2 * DV), 1)
        srow = lax.broadcasted_iota(jnp.int32, (2 * DK, 2 * DV), 0)
        scol = lax.broadcasted_iota(jnp.int32, (2 * DK, 2 * DV), 1)
        in_a = (srow < DK) & (scol < DV)
        in_b = (srow >= DK) & (scol >= DV)
        smask_ref[...] = jnp.where(in_a | in_b, 1.0, 0.0)
        full_c = jnp.full((2 * DK, 2 * DV), c, F32)
        for pp in range(n_pairs):
            lga = lg_ref[2 * pp]
            lgb = lg_ref[2 * pp + 1]
            dmat_ref[pp, 0:c, :] = jnp.where(causal, jnp.exp(relc * lga), 0.0) * scale
            dmat_ref[pp, c:2 * c, :] = jnp.where(causal, jnp.exp(relc * lgb), 0.0) * scale
            kdec_ref[pp] = jnp.exp((c - 1.0 - pos_k) * jnp.where(lane_k < DK, lga, lgb)) * scale
            qdec_ref[pp] = jnp.exp((pos_q + 1.0) * jnp.where(lane_q < DV, lga, lgb))
            sdec_ref[pp] = (jnp.where(in_a, jnp.exp(full_c * lga), 0.0)
                            + jnp.where(in_b, jnp.exp(full_c * lgb), 0.0))
        state_ref[...] = jnp.zeros_like(state_ref)

    def chunk(ci, carry):
        rows = pl.ds(pl.multiple_of(ci * c, c), c)
        lane = lax.broadcasted_iota(jnp.int32, (c, LANES), 1)
        for pp in range(n_pairs):
            qk_cols = slice(pp * LANES, (pp + 1) * LANES)
            v_cols = slice(pp * 2 * DV, (pp + 1) * 2 * DV)
            qc = q_ref[rows, qk_cols]
            kc = k_ref[rows, qk_cols]
            vc = v_ref[rows, v_cols]
            q32 = qc.astype(F32)
            qa = jnp.where(lane < DK, q32, 0.0).astype(BF16)
            qb = jnp.where(lane >= DK, q32, 0.0).astype(BF16)
            qs = jnp.concatenate([qa, qb], axis=0)
            sc = lax.dot_general(qs, kc, _NT, preferred_element_type=F32)
            pm = (sc * dmat_ref[pp]).astype(BF16)
            intra_a = jnp.dot(pm[0:c], vc[:, 0:DV], preferred_element_type=F32)
            intra_b = jnp.dot(pm[c:2 * c], vc[:, DV:2 * DV], preferred_element_type=F32)
            st = state_ref[pp]
            inter = jnp.dot(qc, st.astype(BF16), preferred_element_type=F32)
            out = jnp.concatenate([intra_a, intra_b], axis=1) + inter * qdec_ref[pp]

            kd_t = (kc.astype(F32) * kdec_ref[pp]).T.astype(BF16)
            upd = jnp.dot(kd_t, vc, preferred_element_type=F32)
            state_ref[pp] = st * sdec_ref[pp] + upd * smask_ref[...]

            g = g_ref[:, v_cols]
            act = gate_ref[rows, v_cols].astype(F32)
            halves = []
            for hh in range(2):
                o = out[:, hh * DV:(hh + 1) * DV]
                ms = jnp.mean(o * o, axis=-1, keepdims=True)
                halves.append(o * lax.rsqrt(ms + EPS) * g[:, hh * DV:(hh + 1) * DV])
            o_ref[rows, v_cols] = (jnp.concatenate(halves, axis=1) * act).astype(o_ref.dtype)
        return carry

    lax.fori_loop(0, RET_ROWS // c, chunk, 0)


def _retention(z3, log_gamma, g):
    b, seq, _ = z3.shape
    c = RET_CHUNK
    n_pairs = HEADS // 2
    qk_w = n_pairs * LANES
    v_w = HEADS * DV
    return pl.pallas_call(
        _retention_kernel,
        grid=(b, seq // RET_ROWS),
        in_specs=[
            pl.BlockSpec(memory_space=pltpu.SMEM),
            pl.BlockSpec((None, RET_ROWS, qk_w), lambda i, s: (i, s, COL_QR * LANES // qk_w)),
            pl.BlockSpec((None, RET_ROWS, qk_w), lambda i, s: (i, s, COL_KR * LANES // qk_w)),
            pl.BlockSpec((None, RET_ROWS, v_w), lambda i, s: (i, s, COL_VR * LANES // v_w)),
            pl.BlockSpec((None, RET_ROWS, v_w), lambda i, s: (i, s, COL_GR * LANES // v_w)),
            pl.BlockSpec((1, v_w), lambda i, s: (0, 0)),
        ],
        out_specs=pl.BlockSpec((None, RET_ROWS, v_w), lambda i, s: (i, s, 0)),
        out_shape=jax.ShapeDtypeStruct((b, seq, v_w), BF16),
        scratch_shapes=[
            pltpu.VMEM((n_pairs, 2 * c, c), F32),
            pltpu.VMEM((n_pairs, c, LANES), F32),
            pltpu.VMEM((n_pairs, c, 2 * DV), F32),
            pltpu.VMEM((n_pairs, 2 * DK, 2 * DV), F32),
            pltpu.VMEM((2 * DK, 2 * DV), F32),
            pltpu.VMEM((n_pairs, 2 * DK, 2 * DV), F32),
        ],
        compiler_params=pltpu.CompilerParams(dimension_semantics=("parallel", "arbitrary")),
        name="retention",
    )(log_gamma, z3, z3, z3, z3, g)


def _diffattn_kernel(slope_ref, qt_ref, k_ref, vt_ref, gate_ref, gq_ref, gk_ref,
                     lq1_ref, lk1_ref, lq2_ref, lk2_ref, gs_ref, o_ref,
                     kn_ref, qs_ref, sa_ref, sb_ref, mx_ref, m_ref, acc_ref, *, lam_init):
    tq, tk = ATT_TQ, ATT_TK
    seq = k_ref.shape[0]
    sub = tq // tk
    slope = slope_ref[pl.program_id(1)]
    lam = (jnp.exp(jnp.sum(lq1_ref[...] * lk1_ref[...], axis=-1, keepdims=True))
           - jnp.exp(jnp.sum(lq2_ref[...] * lk2_ref[...], axis=-1, keepdims=True))
           + lam_init)
    same_map = ((lax.broadcasted_iota(jnp.int32, (LANES, LANES), 0) < DK)
                == (lax.broadcasted_iota(jnp.int32, (LANES, LANES), 1) < DK)).astype(F32).astype(BF16)

    def knorm(r, carry):
        rows = pl.ds(pl.multiple_of(r * NORM_ROWS, NORM_ROWS), NORM_ROWS)
        k32 = k_ref[rows, :].astype(F32)
        sq = k32 * k32
        hi = sq.astype(BF16)
        lo = (sq - hi.astype(F32)).astype(BF16)
        ss = (jnp.dot(hi, same_map, preferred_element_type=F32)
              + jnp.dot(lo, same_map, preferred_element_type=F32))
        kn_ref[rows, 0:LANES] = (k32 * lax.rsqrt(ss * (1.0 / DK) + EPS) * gk_ref[...]).astype(BF16)
        pos = r * NORM_ROWS + lax.broadcasted_iota(jnp.int32, (NORM_ROWS, LANES), 0)
        lane = lax.broadcasted_iota(jnp.int32, (NORM_ROWS, LANES), 1)
        lo = pos & (tk - 1)
        part = jnp.where(lane >= 2 * POS_COPIES, 0, jnp.where((lane & 1) == 0, lo, pos - lo))
        kn_ref[rows, LANES:2 * LANES] = part.astype(F32).astype(BF16)
        return carry
    lax.fori_loop(0, seq // NORM_ROWS, knorm, 0, unroll=4)

    c_full = jnp.full((LANES, 2 * tq), slope * LOG2E, F32)
    c0 = c_full.astype(BF16).astype(F32)
    c1 = (c_full - c0).astype(BF16).astype(F32)
    c2 = c_full - c0 - c1
    piece = lax.broadcasted_iota(jnp.int32, (LANES, 2 * tq), 0) >> 1
    alibi_rows = jnp.where(
        piece == 0, c0, jnp.where(piece == 1, c1, jnp.where(piece == 2, c2, 0.0))).astype(BF16)
    ones_rows = jnp.ones((SUM_ROWS, tk), BF16)
    n_q = seq // tq
    n_lanes = 2 * tq
    bufs = (sa_ref, sb_ref)

    def q_prep(qi, carry):
        qt = jnp.concatenate([qt_ref[sub * qi + c] for c in range(sub)], axis=1).astype(F32)
        gq = gq_ref[...]
        halves = []
        for hh in range(2):
            x = qt[hh * DK:(hh + 1) * DK]
            ms = jnp.mean(x * x, axis=0, keepdims=True)
            halves.append(x * lax.rsqrt(ms + EPS) * gq[hh * DK:(hh + 1) * DK] * (DK ** -0.5 * LOG2E))
        zeros = jnp.zeros((DK, tk), F32)
        q1, q2 = halves
        qs_ref[qi, 0:DK, :] = jnp.concatenate(
            [piece for c in range(sub) for piece in (q1[:, c * tk:(c + 1) * tk], zeros)], axis=1).astype(BF16)
        qs_ref[qi, DK:2 * DK, :] = jnp.concatenate(
            [piece for c in range(sub) for piece in (zeros, q2[:, c * tk:(c + 1) * tk])], axis=1).astype(BF16)
        qs_ref[qi, LANES:2 * LANES, :] = alibi_rows
        return carry
    lax.fori_loop(0, n_q, q_prep, 0)

    def scores(j, b, qsel, start=0, with_max=True):
        keys = pl.ds(pl.multiple_of(j * tk, tk), tk)
        sv = jnp.dot(kn_ref[keys, :], qs_ref[qsel, :, start:n_lanes], preferred_element_type=F32)
        bufs[b][:, start:n_lanes] = sv
        if with_max:
            mx_ref[b] = jnp.max(sv, axis=0, keepdims=True)

    scores(0, 0, 0)

    def q_tile(qi, carry):
        m_ref[...] = jnp.full_like(m_ref, NEG)
        acc_ref[...] = jnp.zeros_like(acc_ref)

        def absorb(j, b, start=0, diag=False):
            src = bufs[b]
            lanes = slice(start, n_lanes)
            s = src[:, lanes]
            if diag:
                key = lax.broadcasted_iota(jnp.int32, (tk, 2 * tk), 0)
                qry = lax.broadcasted_iota(jnp.int32, (tk, 2 * tk), 1) & (tk - 1)
                masked = jnp.where(key <= qry, src[:, start:start + 2 * tk], NEG)
                rest = [src[:, start + 2 * tk:n_lanes]] if start + 2 * tk < n_lanes else []
                s = jnp.concatenate([masked] + rest, axis=1)
                m_tile = jnp.max(s, axis=0, keepdims=True)
            else:
                m_tile = mx_ref[b]
            m_old = m_ref[:, lanes]
            m_new = jnp.maximum(m_old, m_tile)
            alpha = jnp.exp2(m_old - m_new)
            p = jnp.exp2(s - m_new).astype(BF16)
            v_aug = jnp.concatenate([vt_ref[j], ones_rows], axis=0)
            acc_ref[:, lanes] = alpha * acc_ref[:, lanes] + jnp.dot(v_aug, p, preferred_element_type=F32)
            m_ref[:, lanes] = m_new

        def quad(mi, c2):
            for u in range(4):
                scores(4 * mi + u + 1, (u + 1) % 2, qi)
                absorb(4 * mi + u, u % 2)
            return c2
        lax.fori_loop(0, qi * (sub // 4), quad, 0)

        first_diag = sub * qi
        for d in range(sub):
            if d + 1 < sub:
                scores(first_diag + d + 1, (d + 1) % 2, qi, start=(d + 1) * 2 * tk, with_max=False)
            else:
                scores(0, 0, jnp.minimum(qi + 1, n_q - 1))
            absorb(first_diag + d, d % 2, start=d * 2 * tk, diag=True)

        inv = 1.0 / acc_ref[DV:DV + 1, :]
        acc = acc_ref[0:DV, :] * inv
        o1 = jnp.concatenate([acc[:, 2 * c * tk:(2 * c + 1) * tk] for c in range(sub)], axis=1)
        o2 = jnp.concatenate([acc[:, (2 * c + 1) * tk:(2 * c + 2) * tk] for c in range(sub)], axis=1)
        ot = o1 - lam * o2
        ms = jnp.mean(ot * ot, axis=0, keepdims=True)
        on = (ot * lax.rsqrt(ms + EPS)).T
        rows = pl.ds(pl.multiple_of(qi * tq, tq), tq)
        act = gate_ref[rows, :].astype(F32)
        o_ref[rows, :] = (on * gs_ref[...] * (1.0 - lam_init) * act).astype(o_ref.dtype)
        return carry

    lax.fori_loop(0, n_q, q_tile, 0)


def _diffattn(z3, t4, slopes, gq, gk, lq1, lk1, lq2, lk2, gs, lam_init):
    b, seq, _ = z3.shape
    tq, tk = ATT_TQ, ATT_TK
    assert tq == 4 * tk
    head_rows = lambda col: pl.BlockSpec((None, seq, LANES), lambda i, h: (i, 0, col + h))
    head_t = lambda off: pl.BlockSpec((None, seq // tk, LANES, tk), lambda i, h: (i, 0, off + h, 0))
    vec = lambda width: pl.BlockSpec((1, width), lambda i, h: (0, 0))
    return pl.pallas_call(
        functools.partial(_diffattn_kernel, lam_init=lam_init),
        grid=(b, HEADS),
        in_specs=[
            pl.BlockSpec(memory_space=pltpu.SMEM),
            head_t(0), head_rows(COL_KD), head_t(HEADS), head_rows(COL_GD),
            pl.BlockSpec((LANES, 1), lambda i, h: (0, 0)),
            vec(LANES), vec(DK), vec(DK), vec(DK), vec(DK),
            pl.BlockSpec((1, DV), lambda i, h: (0, h)),
        ],
        out_specs=pl.BlockSpec((None, seq, DV), lambda i, h: (i, 0, h)),
        out_shape=jax.ShapeDtypeStruct((b, seq, HEADS * DV), BF16),
        scratch_shapes=[
            pltpu.VMEM((seq, 2 * LANES), BF16),
            pltpu.VMEM((seq // tq, 2 * LANES, 2 * tq), BF16),
            pltpu.VMEM((tk, 2 * tq), F32),
            pltpu.VMEM((tk, 2 * tq), F32),
            pltpu.VMEM((2, 1, 2 * tq), F32),
            pltpu.VMEM((1, 2 * tq), F32),
            pltpu.VMEM((DV + SUM_ROWS, 2 * tq), F32),
        ],
        compiler_params=pltpu.CompilerParams(dimension_semantics=("parallel", "parallel")),
        name="diffattn",
    )(slopes, t4, z3, t4, z3, gq, gk, lq1, lk1, lq2, lk2, gs)


def _outproj_kernel(x_ref, or_ref, od_ref, mr_ref, md_ref, wr_ref, wd_ref, wo_ref, out_ref):
    y_r = jnp.dot(or_ref[...], wr_ref[...], preferred_element_type=F32)
    y_d = jnp.dot(od_ref[...], wd_ref[...], preferred_element_type=F32)
    merged = mr_ref[...].astype(F32) * y_r + md_ref[...].astype(F32) * y_d
    out_ref[...] = x_ref[...] + jnp.dot(merged.astype(BF16), wo_ref[...], preferred_element_type=F32)


def _outproj(x2, o_r, o_d, z2, w_r, w_d, w_o, layer):
    m = x2.shape[0]
    rows = lambda col: pl.BlockSpec((OUT_TM, D_MODEL), lambda i: (i, col))
    weight = pl.BlockSpec((None, D_MODEL, D_MODEL), lambda i: (layer, 0, 0))
    return pl.pallas_call(
        _outproj_kernel,
        grid=(m // OUT_TM,),
        in_specs=[rows(0), rows(0), rows(0),
                  rows(COL_MR * LANES // D_MODEL), rows(COL_MD * LANES // D_MODEL),
                  weight, weight, weight],
        out_specs=rows(0),
        out_shape=jax.ShapeDtypeStruct((m, D_MODEL), F32),
        compiler_params=pltpu.CompilerParams(dimension_semantics=("parallel",)),
        name="outproj",
    )(x2, o_r, o_d, z2, z2, w_r, w_d, w_o)


def kernel(x, norm_g, w_in, ret_norm_g, ret_w_o, diff_q_norm_g, diff_k_norm_g,
           diff_lq1, diff_lk1, diff_lq2, diff_lk2, diff_sub_norm_g, diff_w_o, w_out):
    b, seq, d = x.shape
    depth = norm_g.shape[0]
    assert d == D_MODEL and w_in.shape[-1] == N_IN
    assert seq % ATT_TQ == 0 and seq % RET_ROWS == 0 and RET_ROWS % RET_CHUNK == 0 and seq % IN_TM == 0

    assert seq <= ATT_TK * 256
    slopes = jnp.exp2(-8.0 * jnp.arange(1, HEADS + 1, dtype=F32) / HEADS)
    log_gamma = jnp.log1p(-jnp.exp2(-5.0 - jnp.arange(HEADS, dtype=F32)))
    row = lambda v: v.reshape(1, -1).astype(F32)

    w_in_b, w_r_b, w_d_b, w_o_b = (w.astype(BF16) for w in (w_in, ret_w_o, diff_w_o, w_out))

    x2 = x.reshape(b * seq, d)
    for l in range(depth):
        lam_init = 0.8 - 0.6 * math.exp(-0.3 * l)
        z2, t4 = _inproj(x2, row(norm_g[l]), w_in_b, l, b, seq)
        z3 = z2.reshape(b, seq, N_MAIN)
        o_r = _retention(z3, log_gamma, row(ret_norm_g[l]))
        gq_col = jnp.concatenate([diff_q_norm_g[l], diff_q_norm_g[l]]).reshape(-1, 1).astype(F32)
        gk_row = jnp.concatenate([diff_k_norm_g[l], diff_k_norm_g[l]]).reshape(1, -1).astype(F32)
        o_d = _diffattn(z3, t4, slopes, gq_col, gk_row,
                        row(diff_lq1[l]), row(diff_lk1[l]), row(diff_lq2[l]), row(diff_lk2[l]),
                        row(diff_sub_norm_g[l]), lam_init)
        x2 = _outproj(x2, o_r.reshape(b * seq, -1), o_d.reshape(b * seq, -1), z2,
                      w_r_b, w_d_b, w_o_b, l)
    return x2.reshape(b, seq, d)
```

```python
import functools
import math

import jax
import jax.numpy as jnp
from jax import lax
from jax.experimental import pallas as pl
from jax.experimental.pallas import tpu as pltpu

D_MODEL = 1024
HEADS = 8
DK = 64
DV = 128
EPS = 1e-6
LANES = 128
N_IN = 9216
N_MAIN = 7168
N_T = 2048

COL_QR, COL_KR, COL_VR, COL_GR = 0, 4, 8, 16
COL_KD, COL_GD, COL_MR, COL_MD = 24, 32, 40, 48

IN_TM, IN_TN = 512, 1024
IN_VMEM_BYTES = (2 * D_MODEL * (N_MAIN + N_T) + 2 * IN_TM * D_MODEL * 4 + 2 * IN_TM * (N_MAIN + N_T) * 2
                 + IN_TM * D_MODEL * 2 + 2 * IN_TM * IN_TN * 4 + (1 << 20))
NORM_ROWS = 256
RET_CHUNK = 256
RET_ROWS = 1024
ATT_TQ = 1024
ATT_TK = 256
OUT_TM = 512
NEG = -1e30
LOG2E = 1.4426950408889634
POS_COPIES = 3
SUM_ROWS = 16

F32 = jnp.float32
BF16 = jnp.bfloat16
_NT = (((1,), (1,)), ((), ()))


def _silu(v):
    return v * jax.nn.sigmoid(v)


_CHUNK_PLAN = (
    ("z", None),
    ("z", None),
    ("z", _silu),
    ("t", 0),
    ("z", None),
    ("t", 1),
    ("z", _silu),
    ("z", jax.nn.sigmoid),
    ("z", jax.nn.sigmoid),
)


def _inproj_kernel(x_ref, g_ref, w_ref, z_ref, t_ref, h_ref):
    for r in range(IN_TM // NORM_ROWS):
        rows = slice(r * NORM_ROWS, (r + 1) * NORM_ROWS)
        x = x_ref[rows, :]
        ms = jnp.mean(x * x, axis=-1, keepdims=True)
        h_ref[rows, :] = (x * lax.rsqrt(ms + EPS) * g_ref[...]).astype(BF16)
    h = h_ref[...]
    z_chunk = 0
    for n, (kind, arg) in enumerate(_CHUNK_PLAN):
        y = jnp.dot(h, w_ref[:, n * IN_TN:(n + 1) * IN_TN], preferred_element_type=F32)
        if kind == "t":
            yt = y.T.astype(t_ref.dtype)
            for c in range(IN_TM // ATT_TK):
                t_ref[c, arg * IN_TN:(arg + 1) * IN_TN, :] = yt[:, c * ATT_TK:(c + 1) * ATT_TK]
        else:
            if arg is not None:
                y = arg(y)
            z_ref[:, z_chunk * IN_TN:(z_chunk + 1) * IN_TN] = y.astype(z_ref.dtype)
            z_chunk += 1


def _inproj(x2, g, w_all, layer, batch, seq):
    m = x2.shape[0]
    per_batch = seq // IN_TM
    sub = IN_TM // ATT_TK
    resident = dict(pipeline_mode=pl.Buffered(1))
    return pl.pallas_call(
        _inproj_kernel,
        grid=(m // IN_TM,),
        in_specs=[
            pl.BlockSpec((IN_TM, D_MODEL), lambda i: (i, 0)),
            pl.BlockSpec((1, D_MODEL), lambda i: (0, 0)),
            pl.BlockSpec((None, D_MODEL, N_IN), lambda i: (layer, 0, 0), **resident),
        ],
        out_specs=[
            pl.BlockSpec((IN_TM, N_MAIN), lambda i: (i, 0)),
            pl.BlockSpec((None, sub, N_T, ATT_TK), lambda i: (i // per_batch, i % per_batch, 0, 0)),
        ],
        out_shape=[
            jax.ShapeDtypeStruct((m, N_MAIN), BF16),
            jax.ShapeDtypeStruct((batch, seq // ATT_TK, N_T, ATT_TK), BF16),
        ],
        scratch_shapes=[pltpu.VMEM((IN_TM, D_MODEL), BF16)],
        compiler_params=pltpu.CompilerParams(dimension_semantics=("parallel",),
                                             vmem_limit_bytes=IN_VMEM_BYTES),
        name="inproj",
    )(x2, g, w_all)


def _retention_kernel(lg_ref, q_ref, k_ref, v_ref, gate_ref, g_ref, o_ref,
                      dmat_ref, kdec_ref, qdec_ref, sdec_ref, smask_ref, state_ref):
    c = RET_CHUNK
    n_pairs = HEADS // 2
    scale = DK ** -0.5

    @pl.when(pl.program_id(1) == 0)
    def _():
        n = lax.broadcasted_iota(jnp.int32, (c, c), 0)
        mm = lax.broadcasted_iota(jnp.int32, (c, c), 1)
        rel = (n - mm).astype(F32)
        causal = rel >= 0.0
        relc = jnp.maximum(rel, 0.0)
        pos_k = lax.broadcasted_iota(jnp.int32, (c, LANES), 0).astype(F32)
        lane_k = lax.broadcasted_iota(jnp.int32, (c, LANES), 1)
        pos_q = lax.broadcasted_iota(jnp.int32, (c, 2 * DV), 0).astype(F32)
        lane_q = lax.broadcasted_iota(jnp.int32, (c, 2 * DV), 1)
        srow = lax.broadcasted_iota(jnp.int32, (2 * DK, 2 * DV), 0)
        scol = lax.broadcasted_iota(jnp.int32, (2 * DK, 2 * DV), 1)
        in_a = (srow < DK) & (scol < DV)
        in_b = (srow >= DK) & (scol >= DV)
        smask_ref[...] = jnp.where(in_a | in_b, 1.0, 0.0)
        full_c = jnp.full((2 * DK, 2 * DV), c, F32)
        for pp in range(n_pairs):
            lga = lg_ref[2 * pp]
            lgb = lg_ref[2 * pp + 1]
            dmat_ref[pp, 0:c, :] = jnp.where(causal, jnp.exp(relc * lga), 0.0) * scale
            dmat_ref[pp, c:2 * c, :] = jnp.where(causal, jnp.exp(relc * lgb), 0.0) * scale
            kdec_ref[pp] = jnp.exp((c - 1.0 - pos_k) * jnp.where(lane_k < DK, lga, lgb)) * scale
            qdec_ref[pp] = jnp.exp((pos_q + 1.0) * jnp.where(lane_q < DV, lga, lgb))
            sdec_ref[pp] = (jnp.where(in_a, jnp.exp(full_c * lga), 0.0)
                            + jnp.where(in_b, jnp.exp(full_c * lgb), 0.0))
        state_ref[...] = jnp.zeros_like(state_ref)

    def chunk(ci, carry):
        rows = pl.ds(pl.multiple_of(ci * c, c), c)
        lane = lax.broadcasted_iota(jnp.int32, (c, LANES), 1)
        for pp in range(n_pairs):
            qk_cols = slice(pp * LANES, (pp + 1) * LANES)
            v_cols = slice(pp * 2 * DV, (pp + 1) * 2 * DV)
            qc = q_ref[rows, qk_cols]
            kc = k_ref[rows, qk_cols]
            vc = v_ref[rows, v_cols]
            q32 = qc.astype(F32)
            qa = jnp.where(lane < DK, q32, 0.0).astype(BF16)
            qb = jnp.where(lane >= DK, q32, 0.0).astype(BF16)
            qs = jnp.concatenate([qa, qb], axis=0)
            sc = lax.dot_general(qs, kc, _NT, preferred_element_type=F32)
            pm = (sc * dmat_ref[pp]).astype(BF16)
            intra_a = jnp.dot(pm[0:c], vc[:, 0:DV], preferred_element_type=F32)
            intra_b = jnp.dot(pm[c:2 * c], vc[:, DV:2 * DV], preferred_element_type=F32)
            st = state_ref[pp]
            inter = jnp.dot(qc, st.astype(BF16), preferred_element_type=F32)
            out = jnp.concatenate([intra_a, intra_b], axis=1) + inter * qdec_ref[pp]

            kd_t = (kc.astype(F32) * kdec_ref[pp]).T.astype(BF16)
            upd = jnp.dot(kd_t, vc, preferred_element_type=F32)
            state_ref[pp] = st * sdec_ref[pp] + upd * smask_ref[...]

            g = g_ref[:, v_cols]
            act = gate_ref[rows, v_cols].astype(F32)
            halves = []
            for hh in range(2):
                o = out[:, hh * DV:(hh + 1) * DV]
                ms = jnp.mean(o * o, axis=-1, keepdims=True)
                halves.append(o * lax.rsqrt(ms + EPS) * g[:, hh * DV:(hh + 1) * DV])
            o_ref[rows, v_cols] = (jnp.concatenate(halves, axis=1) * act).astype(o_ref.dtype)
        return carry

    lax.fori_loop(0, RET_ROWS // c, chunk, 0)


def _retention(z3, log_gamma, g):
    b, seq, _ = z3.shape
    c = RET_CHUNK
    n_pairs = HEADS // 2
    qk_w = n_pairs * LANES
    v_w = HEADS * DV
    return pl.pallas_call(
        _retention_kernel,
        grid=(b, seq // RET_ROWS),
        in_specs=[
            pl.BlockSpec(memory_space=pltpu.SMEM),
            pl.BlockSpec((None, RET_ROWS, qk_w), lambda i, s: (i, s, COL_QR * LANES // qk_w)),
            pl.BlockSpec((None, RET_ROWS, qk_w), lambda i, s: (i, s, COL_KR * LANES // qk_w)),
            pl.BlockSpec((None, RET_ROWS, v_w), lambda i, s: (i, s, COL_VR * LANES // v_w)),
            pl.BlockSpec((None, RET_ROWS, v_w), lambda i, s: (i, s, COL_GR * LANES // v_w)),
            pl.BlockSpec((1, v_w), lambda i, s: (0, 0)),
        ],
        out_specs=pl.BlockSpec((None, RET_ROWS, v_w), lambda i, s: (i, s, 0)),
        out_shape=jax.ShapeDtypeStruct((b, seq, v_w), BF16),
        scratch_shapes=[
            pltpu.VMEM((n_pairs, 2 * c, c), F32),
            pltpu.VMEM((n_pairs, c, LANES), F32),
            pltpu.VMEM((n_pairs, c, 2 * DV), F32),
            pltpu.VMEM((n_pairs, 2 * DK, 2 * DV), F32),
            pltpu.VMEM((2 * DK, 2 * DV), F32),
            pltpu.VMEM((n_pairs, 2 * DK, 2 * DV), F32),
        ],
        compiler_params=pltpu.CompilerParams(dimension_semantics=("parallel", "arbitrary")),
        name="retention",
    )(log_gamma, z3, z3, z3, z3, g)


def _diffattn_kernel(slope_ref, qt_ref, k_ref, vt_ref, gate_ref, gq_ref, gk_ref,
                     lq1_ref, lk1_ref, lq2_ref, lk2_ref, gs_ref, o_ref,
                     kn_ref, qs_ref, alibi_ref, sa_ref, sb_ref, mx_ref, m_ref, acc_ref, *, lam_init):
    tq, tk = ATT_TQ, ATT_TK
    seq = k_ref.shape[0]
    sub = tq // tk
    slope = slope_ref[pl.program_id(1)]
    lam = (jnp.exp(jnp.sum(lq1_ref[...] * lk1_ref[...], axis=-1, keepdims=True))
           - jnp.exp(jnp.sum(lq2_ref[...] * lk2_ref[...], axis=-1, keepdims=True))
           + lam_init)
    same_map = ((lax.broadcasted_iota(jnp.int32, (LANES, LANES), 0) < DK)
                == (lax.broadcasted_iota(jnp.int32, (LANES, LANES), 1) < DK)).astype(F32).astype(BF16)

    def knorm(r, carry):
        rows = pl.ds(pl.multiple_of(r * NORM_ROWS, NORM_ROWS), NORM_ROWS)
        k32 = k_ref[rows, :].astype(F32)
        sq = k32 * k32
        hi = sq.astype(BF16)
        lo = (sq - hi.astype(F32)).astype(BF16)
        ss = (jnp.dot(hi, same_map, preferred_element_type=F32)
              + jnp.dot(lo, same_map, preferred_element_type=F32))
        kn_ref[rows, 0:LANES] = (k32 * lax.rsqrt(ss * (1.0 / DK) + EPS) * gk_ref[...]).astype(BF16)
        pos = r * NORM_ROWS + lax.broadcasted_iota(jnp.int32, (NORM_ROWS, LANES), 0)
        lane = lax.broadcasted_iota(jnp.int32, (NORM_ROWS, LANES), 1)
        lo = pos & (tk - 1)
        part = jnp.where(lane >= 2 * POS_COPIES, 0, jnp.where((lane & 1) == 0, lo, pos - lo))
        kn_ref[rows, LANES:2 * LANES] = part.astype(F32).astype(BF16)
        return carry
    lax.fori_loop(0, seq // NORM_ROWS, knorm, 0, unroll=8)

    c_full = jnp.full((LANES, 2 * tq), slope * LOG2E, F32)
    c0 = c_full.astype(BF16).astype(F32)
    c1 = (c_full - c0).astype(BF16).astype(F32)
    c2 = c_full - c0 - c1
    piece = lax.broadcasted_iota(jnp.int32, (LANES, 2 * tq), 0) >> 1
    alibi_ref[...] = jnp.where(
        piece == 0, c0, jnp.where(piece == 1, c1, jnp.where(piece == 2, c2, 0.0))).astype(BF16)
    ones_rows = jnp.ones((SUM_ROWS, tk), BF16)
    n_q = seq // tq
    n_lanes = 2 * tq
    bufs = (sa_ref, sb_ref)

    def q_prep(qi, carry):
        qt = jnp.concatenate([qt_ref[sub * qi + c] for c in range(sub)], axis=1).astype(F32)
        gq = gq_ref[...]
        halves = []
        for hh in range(2):
            x = qt[hh * DK:(hh + 1) * DK]
            ms = jnp.mean(x * x, axis=0, keepdims=True)
            halves.append(x * lax.rsqrt(ms + EPS) * gq[hh * DK:(hh + 1) * DK] * (DK ** -0.5 * LOG2E))
        zeros = jnp.zeros((DK, tk), F32)
        q1, q2 = halves
        qs_ref[qi, 0:DK, :] = jnp.concatenate(
            [piece for c in range(sub) for piece in (q1[:, c * tk:(c + 1) * tk], zeros)], axis=1).astype(BF16)
        qs_ref[qi, DK:2 * DK, :] = jnp.concatenate(
            [piece for c in range(sub) for piece in (zeros, q2[:, c * tk:(c + 1) * tk])], axis=1).astype(BF16)
        return carry
    lax.fori_loop(0, n_q, q_prep, 0, unroll=2)

    def scores(j, b, qsel, start=0, with_max=True):
        keys = pl.ds(pl.multiple_of(j * tk, tk), tk)
        rhs = jnp.concatenate([qs_ref[qsel, :, start:n_lanes], alibi_ref[:, start:n_lanes]], axis=0)
        sv = jnp.dot(kn_ref[keys, :], rhs, preferred_element_type=F32)
        bufs[b][:, start:n_lanes] = sv
        if with_max:
            mx_ref[b] = jnp.max(sv, axis=0, keepdims=True)

    scores(0, 0, 0)

    def q_tile(qi, carry):
        m_ref[...] = jnp.full_like(m_ref, NEG)
        acc_ref[...] = jnp.zeros_like(acc_ref)

        def absorb(j, b, start=0, diag=False):
            src = bufs[b]
            lanes = slice(start, n_lanes)
            s = src[:, lanes]
            if diag:
                key = lax.broadcasted_iota(jnp.int32, (tk, 2 * tk), 0)
                qry = lax.broadcasted_iota(jnp.int32, (tk, 2 * tk), 1) & (tk - 1)
                masked = jnp.where(key <= qry, src[:, start:start + 2 * tk], NEG)
                rest = [src[:, start + 2 * tk:n_lanes]] if start + 2 * tk < n_lanes else []
                s = jnp.concatenate([masked] + rest, axis=1)
                m_tile = jnp.max(s, axis=0, keepdims=True)
            else:
                m_tile = mx_ref[b]
            m_old = m_ref[:, lanes]
            m_new = jnp.maximum(m_old, m_tile)
            alpha = jnp.exp2(m_old - m_new)
            p = jnp.exp2(s - m_new).astype(BF16)
            v_aug = jnp.concatenate([vt_ref[j], ones_rows], axis=0)
            acc_ref[:, lanes] = alpha * acc_ref[:, lanes] + jnp.dot(v_aug, p, preferred_element_type=F32)
            m_ref[:, lanes] = m_new

        def quad(mi, c2):
            for u in range(4):
                scores(4 * mi + u + 1, (u + 1) % 2, qi)
                absorb(4 * mi + u, u % 2)
            return c2
        lax.fori_loop(0, qi * (sub // 4), quad, 0)

        first_diag = sub * qi
        for d in range(sub):
            if d + 1 < sub:
                scores(first_diag + d + 1, (d + 1) % 2, qi, start=(d + 1) * 2 * tk, with_max=False)
            else:
                scores(0, 0, jnp.minimum(qi + 1, n_q - 1))
            absorb(first_diag + d, d % 2, start=d * 2 * tk, diag=True)

        inv = 1.0 / acc_ref[DV:DV + 1, :]
        acc = acc_ref[0:DV, :] * inv
        o1 = jnp.concatenate([acc[:, 2 * c * tk:(2 * c + 1) * tk] for c in range(sub)], axis=1)
        o2 = jnp.concatenate([acc[:, (2 * c + 1) * tk:(2 * c + 2) * tk] for c in range(sub)], axis=1)
        ot = o1 - lam * o2
        ms = jnp.mean(ot * ot, axis=0, keepdims=True)
        on = (ot * lax.rsqrt(ms + EPS)).T
        rows = pl.ds(pl.multiple_of(qi * tq, tq), tq)
        act = gate_ref[rows, :].astype(F32)
        o_ref[rows, :] = (on * gs_ref[...] * (1.0 - lam_init) * act).astype(o_ref.dtype)
        return carry

    lax.fori_loop(0, n_q, q_tile, 0)


def _diffattn(z3, t4, slopes, gq, gk, lq1, lk1, lq2, lk2, gs, lam_init):
    b, seq, _ = z3.shape
    tq, tk = ATT_TQ, ATT_TK
    assert tq == 4 * tk
    head_rows = lambda col: pl.BlockSpec((None, seq, LANES), lambda i, h: (i, 0, col + h))
    head_t = lambda off: pl.BlockSpec((None, seq // tk, LANES, tk), lambda i, h: (i, 0, off + h, 0))
    vec = lambda width: pl.BlockSpec((1, width), lambda i, h: (0, 0))
    return pl.pallas_call(
        functools.partial(_diffattn_kernel, lam_init=lam_init),
        grid=(b, HEADS),
        in_specs=[
            pl.BlockSpec(memory_space=pltpu.SMEM),
            head_t(0), head_rows(COL_KD), head_t(HEADS), head_rows(COL_GD),
            pl.BlockSpec((LANES, 1), lambda i, h: (0, 0)),
            vec(LANES), vec(DK), vec(DK), vec(DK), vec(DK),
            pl.BlockSpec((1, DV), lambda i, h: (0, h)),
        ],
        out_specs=pl.BlockSpec((None, seq, DV), lambda i, h: (i, 0, h)),
        out_shape=jax.ShapeDtypeStruct((b, seq, HEADS * DV), BF16),
        scratch_shapes=[
            pltpu.VMEM((seq, 2 * LANES), BF16),
            pltpu.VMEM((seq // tq, LANES, 2 * tq), BF16),
            pltpu.VMEM((LANES, 2 * tq), BF16),
            pltpu.VMEM((tk, 2 * tq), F32),
            pltpu.VMEM((tk, 2 * tq), F32),
            pltpu.VMEM((2, 1, 2 * tq), F32),
            pltpu.VMEM((1, 2 * tq), F32),
            pltpu.VMEM((DV + SUM_ROWS, 2 * tq), F32),
        ],
        compiler_params=pltpu.CompilerParams(dimension_semantics=("parallel", "parallel")),
        name="diffattn",
    )(slopes, t4, z3, t4, z3, gq, gk, lq1, lk1, lq2, lk2, gs)


def _outproj_kernel(x_ref, or_ref, od_ref, mr_ref, md_ref, wr_ref, wd_ref, wo_ref, out_ref):
    y_r = jnp.dot(or_ref[...], wr_ref[...], preferred_element_type=F32)
    y_d = jnp.dot(od_ref[...], wd_ref[...], preferred_element_type=F32)
    merged = mr_ref[...].astype(F32) * y_r + md_ref[...].astype(F32) * y_d
    out_ref[...] = x_ref[...] + jnp.dot(merged.astype(BF16), wo_ref[...], preferred_element_type=F32)


def _outproj(x2, o_r, o_d, z2, w_r, w_d, w_o, layer):
    m = x2.shape[0]
    rows = lambda col: pl.BlockSpec((OUT_TM, D_MODEL), lambda i: (i, col))
    weight = pl.BlockSpec((None, D_MODEL, D_MODEL), lambda i: (layer, 0, 0))
    return pl.pallas_call(
        _outproj_kernel,
        grid=(m // OUT_TM,),
        in_specs=[rows(0), rows(0), rows(0),
                  rows(COL_MR * LANES // D_MODEL), rows(COL_MD * LANES // D_MODEL),
                  weight, weight, weight],
        out_specs=rows(0),
        out_shape=jax.ShapeDtypeStruct((m, D_MODEL), F32),
        compiler_params=pltpu.CompilerParams(dimension_semantics=("parallel",)),
        name="outproj",
    )(x2, o_r, o_d, z2, z2, w_r, w_d, w_o)


def kernel(x, norm_g, w_in, ret_norm_g, ret_w_o, diff_q_norm_g, diff_k_norm_g,
           diff_lq1, diff_lk1, diff_lq2, diff_lk2, diff_sub_norm_g, diff_w_o, w_out):
    b, seq, d = x.shape
    depth = norm_g.shape[0]
    assert d == D_MODEL and w_in.shape[-1] == N_IN
    assert seq % ATT_TQ == 0 and seq % RET_ROWS == 0 and RET_ROWS % RET_CHUNK == 0 and seq % IN_TM == 0

    assert seq <= ATT_TK * 256
    slopes = jnp.exp2(-8.0 * jnp.arange(1, HEADS + 1, dtype=F32) / HEADS)
    log_gamma = jnp.log1p(-jnp.exp2(-5.0 - jnp.arange(HEADS, dtype=F32)))
    row = lambda v: v.reshape(1, -1).astype(F32)

    w_in_b, w_r_b, w_d_b, w_o_b = (w.astype(BF16) for w in (w_in, ret_w_o, diff_w_o, w_out))

    x2 = x.reshape(b * seq, d)
    for l in range(depth):
        lam_init = 0.8 - 0.6 * math.exp(-0.3 * l)
        z2, t4 = _inproj(x2, row(norm_g[l]), w_in_b, l, b, seq)
        z3 = z2.reshape(b, seq, N_MAIN)
        o_r = _retention(z3, log_gamma, row(ret_norm_g[l]))
        gq_col = jnp.concatenate([diff_q_norm_g[l], diff_q_norm_g[l]]).reshape(-1, 1).astype(F32)
        gk_row = jnp.concatenate([diff_k_norm_g[l], diff_k_norm_g[l]]).reshape(1, -1).astype(F32)
        o_d = _diffattn(z3, t4, slopes, gq_col, gk_row,
                        row(diff_lq1[l]), row(diff_lk1[l]), row(diff_lq2[l]), row(diff_lk2[l]),
                        row(diff_sub_norm_g[l]), lam_init)
        x2 = _outproj(x2, o_r.reshape(b * seq, -1), o_d.reshape(b * seq, -1), z2,
                      w_r_b, w_d_b, w_o_b, l)
    return x2.reshape(b, seq, d)
```

```python
import functools
import math

import jax
import jax.numpy as jnp
from jax import lax
from jax.experimental import pallas as pl
from jax.experimental.pallas import tpu as pltpu

D_MODEL = 1024
HEADS = 8
DK = 64
DV = 128
EPS = 1e-6
LANES = 128
N_IN = 9216
N_MAIN = 7168
N_T = 2048

COL_QR, COL_KR, COL_VR, COL_GR = 0, 4, 8, 16
COL_KD, COL_GD, COL_MR, COL_MD = 24, 32, 40, 48

IN_TM, IN_TN = 512, 1024
IN_VMEM_BYTES = (2 * D_MODEL * (N_MAIN + N_T) + 2 * IN_TM * D_MODEL * 4 + 2 * IN_TM * (N_MAIN + N_T) * 2
                 + IN_TM * D_MODEL * 2 + 2 * IN_TM * IN_TN * 4 + (1 << 20))
NORM_ROWS = 256
RET_CHUNK = 256
RET_ROWS = 1024
ATT_TQ = 1024
ATT_TK = 256
OUT_TM = 512
NEG = -1e30
LOG2E = 1.4426950408889634
POS_COPIES = 3
SUM_ROWS = 16

F32 = jnp.float32
BF16 = jnp.bfloat16
_NT = (((1,), (1,)), ((), ()))


def _silu(v):
    return v * jax.nn.sigmoid(v)


_CHUNK_PLAN = (
    ("z", None),
    ("z", None),
    ("z", _silu),
    ("t", 0),
    ("z", None),
    ("t", 1),
    ("z", _silu),
    ("z", jax.nn.sigmoid),
    ("z", jax.nn.sigmoid),
)


def _inproj_kernel(x_ref, g_ref, w_ref, z_ref, t_ref, h_ref):
    for r in range(IN_TM // NORM_ROWS):
        rows = slice(r * NORM_ROWS, (r + 1) * NORM_ROWS)
        x = x_ref[rows, :]
        ms = jnp.mean(x * x, axis=-1, keepdims=True)
        h_ref[rows, :] = (x * lax.rsqrt(ms + EPS) * g_ref[...]).astype(BF16)
    h = h_ref[...]
    z_chunk = 0
    for n, (kind, arg) in enumerate(_CHUNK_PLAN):
        y = jnp.dot(h, w_ref[:, n * IN_TN:(n + 1) * IN_TN], preferred_element_type=F32)
        if kind == "t":
            yt = y.T.astype(t_ref.dtype)
            for c in range(IN_TM // ATT_TK):
                t_ref[c, arg * IN_TN:(arg + 1) * IN_TN, :] = yt[:, c * ATT_TK:(c + 1) * ATT_TK]
        else:
            if arg is not None:
                y = arg(y)
            z_ref[:, z_chunk * IN_TN:(z_chunk + 1) * IN_TN] = y.astype(z_ref.dtype)
            z_chunk += 1


def _inproj(x2, g, w_all, layer, batch, seq):
    m = x2.shape[0]
    per_batch = seq // IN_TM
    sub = IN_TM // ATT_TK
    resident = dict(pipeline_mode=pl.Buffered(1))
    return pl.pallas_call(
        _inproj_kernel,
        grid=(m // IN_TM,),
        in_specs=[
            pl.BlockSpec((IN_TM, D_MODEL), lambda i: (i, 0)),
            pl.BlockSpec((1, D_MODEL), lambda i: (0, 0)),
            pl.BlockSpec((None, D_MODEL, N_IN), lambda i: (layer, 0, 0), **resident),
        ],
        out_specs=[
            pl.BlockSpec((IN_TM, N_MAIN), lambda i: (i, 0)),
            pl.BlockSpec((None, sub, N_T, ATT_TK), lambda i: (i // per_batch, i % per_batch, 0, 0)),
        ],
        out_shape=[
            jax.ShapeDtypeStruct((m, N_MAIN), BF16),
            jax.ShapeDtypeStruct((batch, seq // ATT_TK, N_T, ATT_TK), BF16),
        ],
        scratch_shapes=[pltpu.VMEM((IN_TM, D_MODEL), BF16)],
        compiler_params=pltpu.CompilerParams(dimension_semantics=("parallel",),
                                             vmem_limit_bytes=IN_VMEM_BYTES),
        name="inproj",
    )(x2, g, w_all)


def _retention_kernel(lg_ref, q_ref, k_ref, v_ref, gate_ref, g_ref, o_ref,
                      dmat_ref, kdec_ref, qdec_ref, sdec_ref, smask_ref, state_ref):
    c = RET_CHUNK
    n_pairs = HEADS // 2
    scale = DK ** -0.5

    @pl.when(pl.program_id(1) == 0)
    def _():
        n = lax.broadcasted_iota(jnp.int32, (c, c), 0)
        mm = lax.broadcasted_iota(jnp.int32, (c, c), 1)
        rel = (n - mm).astype(F32)
        causal = rel >= 0.0
        relc = jnp.maximum(rel, 0.0)
        pos_k = lax.broadcasted_iota(jnp.int32, (c, LANES), 0).astype(F32)
        lane_k = lax.broadcasted_iota(jnp.int32, (c, LANES), 1)
        pos_q = lax.broadcasted_iota(jnp.int32, (c, 2 * DV), 0).astype(F32)
        lane_q = lax.broadcasted_iota(jnp.int32, (c, 2 * DV), 1)
        srow = lax.broadcasted_iota(jnp.int32, (2 * DK, 2 * DV), 0)
        scol = lax.broadcasted_iota(jnp.int32, (2 * DK, 2 * DV), 1)
        in_a = (srow < DK) & (scol < DV)
        in_b = (srow >= DK) & (scol >= DV)
        smask_ref[...] = jnp.where(in_a | in_b, 1.0, 0.0)
        full_c = jnp.full((2 * DK, 2 * DV), c, F32)
        for pp in range(n_pairs):
            lga = lg_ref[2 * pp]
            lgb = lg_ref[2 * pp + 1]
            dmat_ref[pp, 0:c, :] = jnp.where(causal, jnp.exp(relc * lga), 0.0) * scale
            dmat_ref[pp, c:2 * c, :] = jnp.where(causal, jnp.exp(relc * lgb), 0.0) * scale
            kdec_ref[pp] = jnp.exp((c - 1.0 - pos_k) * jnp.where(lane_k < DK, lga, lgb)) * scale
            qdec_ref[pp] = jnp.exp((pos_q + 1.0) * jnp.where(lane_q < DV, lga, lgb))
            sdec_ref[pp] = (jnp.where(in_a, jnp.exp(full_c * lga), 0.0)
                            + jnp.where(in_b, jnp.exp(full_c * lgb), 0.0))
        state_ref[...] = jnp.zeros_like(state_ref)

    def chunk(ci, carry):
        rows = pl.ds(pl.multiple_of(ci * c, c), c)
        lane = lax.broadcasted_iota(jnp.int32, (c, LANES), 1)
        for pp in range(n_pairs):
            qk_cols = slice(pp * LANES, (pp + 1) * LANES)
            v_cols = slice(pp * 2 * DV, (pp + 1) * 2 * DV)
            qc = q_ref[rows, qk_cols]
            kc = k_ref[rows, qk_cols]
            vc = v_ref[rows, v_cols]
            q32 = qc.astype(F32)
            qa = jnp.where(lane < DK, q32, 0.0).astype(BF16)
            qb = jnp.where(lane >= DK, q32, 0.0).astype(BF16)
            qs = jnp.concatenate([qa, qb], axis=0)
            sc = lax.dot_general(qs, kc, _NT, preferred_element_type=F32)
            pm = (sc * dmat_ref[pp]).astype(BF16)
            intra_a = jnp.dot(pm[0:c], vc[:, 0:DV], preferred_element_type=F32)
            intra_b = jnp.dot(pm[c:2 * c], vc[:, DV:2 * DV], preferred_element_type=F32)
            st = state_ref[pp]
            inter = jnp.dot(qc, st.astype(BF16), preferred_element_type=F32)
            out = jnp.concatenate([intra_a, intra_b], axis=1) + inter * qdec_ref[pp]

            kd_t = (kc.astype(F32) * kdec_ref[pp]).T.astype(BF16)
            upd = jnp.dot(kd_t, vc, preferred_element_type=F32)
            state_ref[pp] = st * sdec_ref[pp] + upd * smask_ref[...]

            g = g_ref[:, v_cols]
            act = gate_ref[rows, v_cols].astype(F32)
            halves = []
            for hh in range(2):
                o = out[:, hh * DV:(hh + 1) * DV]
                ms = jnp.mean(o * o, axis=-1, keepdims=True)
                halves.append(o * lax.rsqrt(ms + EPS) * g[:, hh * DV:(hh + 1) * DV])
            o_ref[rows, v_cols] = (jnp.concatenate(halves, axis=1) * act).astype(o_ref.dtype)
        return carry

    lax.fori_loop(0, RET_ROWS // c, chunk, 0)


def _retention(z3, log_gamma, g):
    b, seq, _ = z3.shape
    c = RET_CHUNK
    n_pairs = HEADS // 2
    qk_w = n_pairs * LANES
    v_w = HEADS * DV
    return pl.pallas_call(
        _retention_kernel,
        grid=(b, seq // RET_ROWS),
        in_specs=[
            pl.BlockSpec(memory_space=pltpu.SMEM),
            pl.BlockSpec((None, RET_ROWS, qk_w), lambda i, s: (i, s, COL_QR * LANES // qk_w)),
            pl.BlockSpec((None, RET_ROWS, qk_w), lambda i, s: (i, s, COL_KR * LANES // qk_w)),
            pl.BlockSpec((None, RET_ROWS, v_w), lambda i, s: (i, s, COL_VR * LANES // v_w)),
            pl.BlockSpec((None, RET_ROWS, v_w), lambda i, s: (i, s, COL_GR * LANES // v_w)),
            pl.BlockSpec((1, v_w), lambda i, s: (0, 0)),
        ],
        out_specs=pl.BlockSpec((None, RET_ROWS, v_w), lambda i, s: (i, s, 0)),
        out_shape=jax.ShapeDtypeStruct((b, seq, v_w), BF16),
        scratch_shapes=[
            pltpu.VMEM((n_pairs, 2 * c, c), F32),
            pltpu.VMEM((n_pairs, c, LANES), F32),
            pltpu.VMEM((n_pairs, c, 2 * DV), F32),
            pltpu.VMEM((n_pairs, 2 * DK, 2 * DV), F32),
            pltpu.VMEM((2 * DK, 2 * DV), F32),
            pltpu.VMEM((n_pairs, 2 * DK, 2 * DV), F32),
        ],
        compiler_params=pltpu.CompilerParams(dimension_semantics=("parallel", "arbitrary")),
        name="retention",
    )(log_gamma, z3, z3, z3, z3, g)


def _diffattn_kernel(slope_ref, qt_ref, k_ref, vt_ref, gate_ref, gq_ref, gk_ref,
                     lq1_ref, lk1_ref, lq2_ref, lk2_ref, gs_ref, o_ref,
                     kn_ref, qs_ref, alibi_ref, sa_ref, sb_ref, mx_ref, m_ref, acc_ref, *, lam_init):
    tq, tk = ATT_TQ, ATT_TK
    seq = k_ref.shape[0]
    sub = tq // tk
    slope = slope_ref[pl.program_id(1)]
    lam = (jnp.exp(jnp.sum(lq1_ref[...] * lk1_ref[...], axis=-1, keepdims=True))
           - jnp.exp(jnp.sum(lq2_ref[...] * lk2_ref[...], axis=-1, keepdims=True))
           + lam_init)
    same_map = ((lax.broadcasted_iota(jnp.int32, (LANES, LANES), 0) < DK)
                == (lax.broadcasted_iota(jnp.int32, (LANES, LANES), 1) < DK)).astype(F32).astype(BF16)

    def knorm(r, carry):
        rows = pl.ds(pl.multiple_of(r * NORM_ROWS, NORM_ROWS), NORM_ROWS)
        k32 = k_ref[rows, :].astype(F32)
        sq = k32 * k32
        hi = sq.astype(BF16)
        lo = (sq - hi.astype(F32)).astype(BF16)
        ss = (jnp.dot(hi, same_map, preferred_element_type=F32)
              + jnp.dot(lo, same_map, preferred_element_type=F32))
        kn_ref[rows, 0:LANES] = (k32 * lax.rsqrt(ss * (1.0 / DK) + EPS) * gk_ref[...]).astype(BF16)
        pos = r * NORM_ROWS + lax.broadcasted_iota(jnp.int32, (NORM_ROWS, LANES), 0)
        lane = lax.broadcasted_iota(jnp.int32, (NORM_ROWS, LANES), 1)
        lo = pos & (tk - 1)
        part = jnp.where(lane >= 2 * POS_COPIES, 0, jnp.where((lane & 1) == 0, lo, pos - lo))
        kn_ref[rows, LANES:2 * LANES] = part.astype(F32).astype(BF16)
        return carry
    lax.fori_loop(0, seq // NORM_ROWS, knorm, 0, unroll=8)

    c_full = jnp.full((LANES, 2 * tq), slope * LOG2E, F32)
    c0 = c_full.astype(BF16).astype(F32)
    c1 = (c_full - c0).astype(BF16).astype(F32)
    c2 = c_full - c0 - c1
    piece = lax.broadcasted_iota(jnp.int32, (LANES, 2 * tq), 0) >> 1
    alibi_ref[:, 0:2 * tq] = jnp.where(
        piece == 0, c0, jnp.where(piece == 1, c1, jnp.where(piece == 2, c2, 0.0))).astype(BF16)
    ones_rows = jnp.ones((SUM_ROWS, tk), BF16)
    n_q = seq // tq
    n_lanes = 2 * tq
    bufs = (sa_ref, sb_ref)

    def q_prep(qi, carry):
        qt = jnp.concatenate([qt_ref[sub * qi + c] for c in range(sub)], axis=1).astype(F32)
        gq = gq_ref[...]
        halves = []
        for hh in range(2):
            x = qt[hh * DK:(hh + 1) * DK]
            ms = jnp.mean(x * x, axis=0, keepdims=True)
            halves.append(x * lax.rsqrt(ms + EPS) * gq[hh * DK:(hh + 1) * DK] * (DK ** -0.5 * LOG2E))
        zeros = jnp.zeros((DK, tk), F32)
        q1, q2 = halves
        qs_ref[qi, 0:DK, 0:n_lanes] = jnp.concatenate(
            [piece for c in range(sub) for piece in (q1[:, c * tk:(c + 1) * tk], zeros)], axis=1).astype(BF16)
        qs_ref[qi, DK:2 * DK, 0:n_lanes] = jnp.concatenate(
            [piece for c in range(sub) for piece in (zeros, q2[:, c * tk:(c + 1) * tk])], axis=1).astype(BF16)
        return carry
    lax.fori_loop(0, n_q, q_prep, 0, unroll=2)

    def scores(j, b, qsel, start=0, with_max=True):
        keys = pl.ds(pl.multiple_of(j * tk, tk), tk)
        rhs = jnp.concatenate([qs_ref[qsel, :, start:n_lanes], alibi_ref[:, start:n_lanes]], axis=0)
        sv = jnp.dot(kn_ref[keys, :], rhs, preferred_element_type=F32)
        bufs[b][:, start:n_lanes] = sv
        if with_max:
            mx_ref[b] = jnp.max(sv, axis=0, keepdims=True)

    scores(0, 0, 0)

    def q_tile(qi, carry):
        m_ref[...] = jnp.full_like(m_ref, NEG)
        acc_ref[:, 0:n_lanes] = jnp.zeros((DV + SUM_ROWS, n_lanes), F32)

        def absorb(j, b, start=0, diag=False):
            src = bufs[b]
            lanes = slice(start, n_lanes)
            s = src[:, lanes]
            if diag:
                key = lax.broadcasted_iota(jnp.int32, (tk, 2 * tk), 0)
                qry = lax.broadcasted_iota(jnp.int32, (tk, 2 * tk), 1) & (tk - 1)
                masked = jnp.where(key <= qry, src[:, start:start + 2 * tk], NEG)
                rest = [src[:, start + 2 * tk:n_lanes]] if start + 2 * tk < n_lanes else []
                s = jnp.concatenate([masked] + rest, axis=1)
                m_tile = jnp.max(s, axis=0, keepdims=True)
            else:
                m_tile = mx_ref[b]
            m_old = m_ref[:, lanes]
            m_new = jnp.maximum(m_old, m_tile)
            alpha = jnp.exp2(m_old - m_new)
            p = jnp.exp2(s - m_new).astype(BF16)
            v_aug = jnp.concatenate([vt_ref[j], ones_rows], axis=0)
            acc_ref[:, lanes] = alpha * acc_ref[:, lanes] + jnp.dot(v_aug, p, preferred_element_type=F32)
            m_ref[:, lanes] = m_new

        def quad(mi, c2):
            for u in range(4):
                scores(4 * mi + u + 1, (u + 1) % 2, qi)
                absorb(4 * mi + u, u % 2)
            return c2
        lax.fori_loop(0, qi * (sub // 4), quad, 0)

        first_diag = sub * qi
        for d in range(sub):
            if d + 1 < sub:
                scores(first_diag + d + 1, (d + 1) % 2, qi, start=(d + 1) * 2 * tk, with_max=False)
            else:
                scores(0, 0, jnp.minimum(qi + 1, n_q - 1))
            absorb(first_diag + d, d % 2, start=d * 2 * tk, diag=True)

        inv = 1.0 / acc_ref[DV:DV + 1, 0:n_lanes]
        acc = acc_ref[0:DV, 0:n_lanes] * inv
        o1 = jnp.concatenate([acc[:, 2 * c * tk:(2 * c + 1) * tk] for c in range(sub)], axis=1)
        o2 = jnp.concatenate([acc[:, (2 * c + 1) * tk:(2 * c + 2) * tk] for c in range(sub)], axis=1)
        ot = o1 - lam * o2
        ms = jnp.mean(ot * ot, axis=0, keepdims=True)
        on = (ot * lax.rsqrt(ms + EPS)).T
        rows = pl.ds(pl.multiple_of(qi * tq, tq), tq)
        act = gate_ref[rows, :].astype(F32)
        o_ref[rows, :] = (on * gs_ref[...] * (1.0 - lam_init) * act).astype(o_ref.dtype)
        return carry

    lax.fori_loop(0, n_q, q_tile, 0)


def _diffattn(z3, t4, slopes, gq, gk, lq1, lk1, lq2, lk2, gs, lam_init):
    b, seq, _ = z3.shape
    tq, tk = ATT_TQ, ATT_TK
    assert tq == 4 * tk
    head_rows = lambda col: pl.BlockSpec((None, seq, LANES), lambda i, h: (i, 0, col + h))
    head_t = lambda off: pl.BlockSpec((None, seq // tk, LANES, tk), lambda i, h: (i, 0, off + h, 0))
    vec = lambda width: pl.BlockSpec((1, width), lambda i, h: (0, 0))
    return pl.pallas_call(
        functools.partial(_diffattn_kernel, lam_init=lam_init),
        grid=(b, HEADS),
        in_specs=[
            pl.BlockSpec(memory_space=pltpu.SMEM),
            head_t(0), head_rows(COL_KD), head_t(HEADS), head_rows(COL_GD),
            pl.BlockSpec((LANES, 1), lambda i, h: (0, 0)),
            vec(LANES), vec(DK), vec(DK), vec(DK), vec(DK),
            pl.BlockSpec((1, DV), lambda i, h: (0, h)),
        ],
        out_specs=pl.BlockSpec((None, seq, DV), lambda i, h: (i, 0, h)),
        out_shape=jax.ShapeDtypeStruct((b, seq, HEADS * DV), BF16),
        scratch_shapes=[
            pltpu.VMEM((seq, 2 * LANES), BF16),
            pltpu.VMEM((seq // tq, LANES, 2 * tq + LANES), BF16),
            pltpu.VMEM((LANES, 2 * tq + LANES), BF16),
            pltpu.VMEM((tk, 2 * tq + LANES), F32),
            pltpu.VMEM((tk, 2 * tq + LANES), F32),
            pltpu.VMEM((2, 1, 2 * tq), F32),
            pltpu.VMEM((1, 2 * tq), F32),
            pltpu.VMEM((DV + SUM_ROWS, 2 * tq + LANES), F32),
        ],
        compiler_params=pltpu.CompilerParams(dimension_semantics=("parallel", "parallel")),
        name="diffattn",
    )(slopes, t4, z3, t4, z3, gq, gk, lq1, lk1, lq2, lk2, gs)


def _outproj_kernel(x_ref, or_ref, od_ref, mr_ref, md_ref, wr_ref, wd_ref, wo_ref, out_ref):
    y_r = jnp.dot(or_ref[...], wr_ref[...], preferred_element_type=F32)
    y_d = jnp.dot(od_ref[...], wd_ref[...], preferred_element_type=F32)
    merged = mr_ref[...].astype(F32) * y_r + md_ref[...].astype(F32) * y_d
    out_ref[...] = x_ref[...] + jnp.dot(merged.astype(BF16), wo_ref[...], preferred_element_type=F32)


def _outproj(x2, o_r, o_d, z2, w_r, w_d, w_o, layer):
    m = x2.shape[0]
    rows = lambda col: pl.BlockSpec((OUT_TM, D_MODEL), lambda i: (i, col))
    weight = pl.BlockSpec((None, D_MODEL, D_MODEL), lambda i: (layer, 0, 0))
    return pl.pallas_call(
        _outproj_kernel,
        grid=(m // OUT_TM,),
        in_specs=[rows(0), rows(0), rows(0),
                  rows(COL_MR * LANES // D_MODEL), rows(COL_MD * LANES // D_MODEL),
                  weight, weight, weight],
        out_specs=rows(0),
        out_shape=jax.ShapeDtypeStruct((m, D_MODEL), F32),
        compiler_params=pltpu.CompilerParams(dimension_semantics=("parallel",)),
        name="outproj",
    )(x2, o_r, o_d, z2, z2, w_r, w_d, w_o)


def kernel(x, norm_g, w_in, ret_norm_g, ret_w_o, diff_q_norm_g, diff_k_norm_g,
           diff_lq1, diff_lk1, diff_lq2, diff_lk2, diff_sub_norm_g, diff_w_o, w_out):
    b, seq, d = x.shape
    depth = norm_g.shape[0]
    assert d == D_MODEL and w_in.shape[-1] == N_IN
    assert seq % ATT_TQ == 0 and seq % RET_ROWS == 0 and RET_ROWS % RET_CHUNK == 0 and seq % IN_TM == 0

    assert seq <= ATT_TK * 256
    slopes = jnp.exp2(-8.0 * jnp.arange(1, HEADS + 1, dtype=F32) / HEADS)
    log_gamma = jnp.log1p(-jnp.exp2(-5.0 - jnp.arange(HEADS, dtype=F32)))
    row = lambda v: v.reshape(1, -1).astype(F32)

    w_in_b, w_r_b, w_d_b, w_o_b = (w.astype(BF16) for w in (w_in, ret_w_o, diff_w_o, w_out))

    x2 = x.reshape(b * seq, d)
    for l in range(depth):
        lam_init = 0.8 - 0.6 * math.exp(-0.3 * l)
        z2, t4 = _inproj(x2, row(norm_g[l]), w_in_b, l, b, seq)
        z3 = z2.reshape(b, seq, N_MAIN)
        o_r = _retention(z3, log_gamma, row(ret_norm_g[l]))
        gq_col = jnp.concatenate([diff_q_norm_g[l], diff_q_norm_g[l]]).reshape(-1, 1).astype(F32)
        gk_row = jnp.concatenate([diff_k_norm_g[l], diff_k_norm_g[l]]).reshape(1, -1).astype(F32)
        o_d = _diffattn(z3, t4, slopes, gq_col, gk_row,
                        row(diff_lq1[l]), row(diff_lk1[l]), row(diff_lq2[l]), row(diff_lk2[l]),
                        row(diff_sub_norm_g[l]), lam_init)
        x2 = _outproj(x2, o_r.reshape(b * seq, -1), o_d.reshape(b * seq, -1), z2,
                      w_r_b, w_d_b, w_o_b, l)
    return x2.reshape(b, seq, d)
```

```python
import functools
import math

import jax
import jax.numpy as jnp
from jax import lax
from jax.experimental import pallas as pl
from jax.experimental.pallas import tpu as pltpu

D_MODEL = 1024
HEADS = 8
DK = 64
DV = 128
EPS = 1e-6
LANES = 128
N_IN = 9216
N_MAIN = 7168
N_T = 2048

COL_QR, COL_KR, COL_VR, COL_GR = 0, 4, 8, 16
COL_KD, COL_GD, COL_MR, COL_MD = 24, 32, 40, 48

IN_TM, IN_TN = 512, 1024
IN_VMEM_BYTES = (2 * D_MODEL * (N_MAIN + N_T) + 2 * IN_TM * D_MODEL * 4 + 2 * IN_TM * (N_MAIN + N_T) * 2
                 + IN_TM * D_MODEL * 2 + 2 * IN_TM * IN_TN * 4 + (1 << 20))
NORM_ROWS = 256
RET_CHUNK = 256
RET_ROWS = 1024
ATT_TQ = 1024
ATT_TK = 256
OUT_TM = 512
NEG = -1e30
LOG2E = 1.4426950408889634
POS_COPIES = 3
SUM_ROWS = 16

F32 = jnp.float32
BF16 = jnp.bfloat16
_NT = (((1,), (1,)), ((), ()))


def _silu(v):
    return v * jax.nn.sigmoid(v)


_CHUNK_PLAN = (
    ("z", None),
    ("z", None),
    ("z", _silu),
    ("t", 0),
    ("z", None),
    ("t", 1),
    ("z", _silu),
    ("z", jax.nn.sigmoid),
    ("z", jax.nn.sigmoid),
)


def _inproj_kernel(x_ref, g_ref, w_ref, z_ref, t_ref, h_ref):
    for r in range(IN_TM // NORM_ROWS):
        rows = slice(r * NORM_ROWS, (r + 1) * NORM_ROWS)
        x = x_ref[rows, :]
        ms = jnp.mean(x * x, axis=-1, keepdims=True)
        h_ref[rows, 0:D_MODEL] = (x * lax.rsqrt(ms + EPS) * g_ref[...]).astype(BF16)
    h = h_ref[:, 0:D_MODEL]
    z_chunk = 0
    for n, (kind, arg) in enumerate(_CHUNK_PLAN):
        y = jnp.dot(h, w_ref[:, n * IN_TN:(n + 1) * IN_TN], preferred_element_type=F32)
        if kind == "t":
            yt = y.T.astype(t_ref.dtype)
            for c in range(IN_TM // ATT_TK):
                t_ref[c, arg * IN_TN:(arg + 1) * IN_TN, :] = yt[:, c * ATT_TK:(c + 1) * ATT_TK]
        else:
            if arg is not None:
                y = arg(y)
            z_ref[:, z_chunk * IN_TN:(z_chunk + 1) * IN_TN] = y.astype(z_ref.dtype)
            z_chunk += 1


def _inproj(x2, g, w_all, layer, batch, seq):
    m = x2.shape[0]
    per_batch = seq // IN_TM
    sub = IN_TM // ATT_TK
    resident = dict(pipeline_mode=pl.Buffered(1))
    return pl.pallas_call(
        _inproj_kernel,
        grid=(m // IN_TM,),
        in_specs=[
            pl.BlockSpec((IN_TM, D_MODEL), lambda i: (i, 0)),
            pl.BlockSpec((1, D_MODEL), lambda i: (0, 0)),
            pl.BlockSpec((None, D_MODEL, N_IN), lambda i: (layer, 0, 0), **resident),
        ],
        out_specs=[
            pl.BlockSpec((IN_TM, N_MAIN), lambda i: (i, 0)),
            pl.BlockSpec((None, sub, N_T, ATT_TK), lambda i: (i // per_batch, i % per_batch, 0, 0)),
        ],
        out_shape=[
            jax.ShapeDtypeStruct((m, N_MAIN), BF16),
            jax.ShapeDtypeStruct((batch, seq // ATT_TK, N_T, ATT_TK), BF16),
        ],
        scratch_shapes=[pltpu.VMEM((IN_TM, D_MODEL + LANES), BF16)],
        compiler_params=pltpu.CompilerParams(dimension_semantics=("parallel",),
                                             vmem_limit_bytes=IN_VMEM_BYTES),
        name="inproj",
    )(x2, g, w_all)


def _retention_kernel(lg_ref, q_ref, k_ref, v_ref, gate_ref, g_ref, o_ref,
                      dmat_ref, kdec_ref, qdec_ref, sdec_ref, smask_ref, state_ref):
    c = RET_CHUNK
    n_pairs = HEADS // 2
    scale = DK ** -0.5

    @pl.when(pl.program_id(1) == 0)
    def _():
        n = lax.broadcasted_iota(jnp.int32, (c, c), 0)
        mm = lax.broadcasted_iota(jnp.int32, (c, c), 1)
        rel = (n - mm).astype(F32)
        causal = rel >= 0.0
        relc = jnp.maximum(rel, 0.0)
        pos_k = lax.broadcasted_iota(jnp.int32, (c, LANES), 0).astype(F32)
        lane_k = lax.broadcasted_iota(jnp.int32, (c, LANES), 1)
        pos_q = lax.broadcasted_iota(jnp.int32, (c, 2 * DV), 0).astype(F32)
        lane_q = lax.broadcasted_iota(jnp.int32, (c, 2 * DV), 1)
        srow = lax.broadcasted_iota(jnp.int32, (2 * DK, 2 * DV), 0)
        scol = lax.broadcasted_iota(jnp.int32, (2 * DK, 2 * DV), 1)
        in_a = (srow < DK) & (scol < DV)
        in_b = (srow >= DK) & (scol >= DV)
        smask_ref[...] = jnp.where(in_a | in_b, 1.0, 0.0)
        full_c = jnp.full((2 * DK, 2 * DV), c, F32)
        for pp in range(n_pairs):
            lga = lg_ref[2 * pp]
            lgb = lg_ref[2 * pp + 1]
            dmat_ref[pp, 0:c, :] = jnp.where(causal, jnp.exp(relc * lga), 0.0) * scale
            dmat_ref[pp, c:2 * c, :] = jnp.where(causal, jnp.exp(relc * lgb), 0.0) * scale
            kdec_ref[pp] = jnp.exp((c - 1.0 - pos_k) * jnp.where(lane_k < DK, lga, lgb)) * scale
            qdec_ref[pp] = jnp.exp((pos_q + 1.0) * jnp.where(lane_q < DV, lga, lgb))
            sdec_ref[pp] = (jnp.where(in_a, jnp.exp(full_c * lga), 0.0)
                            + jnp.where(in_b, jnp.exp(full_c * lgb), 0.0))
        state_ref[...] = jnp.zeros_like(state_ref)

    def chunk(ci, carry):
        rows = pl.ds(pl.multiple_of(ci * c, c), c)
        lane = lax.broadcasted_iota(jnp.int32, (c, LANES), 1)
        for pp in range(n_pairs):
            qk_cols = slice(pp * LANES, (pp + 1) * LANES)
            v_cols = slice(pp * 2 * DV, (pp + 1) * 2 * DV)
            qc = q_ref[rows, qk_cols]
            kc = k_ref[rows, qk_cols]
            vc = v_ref[rows, v_cols]
            q32 = qc.astype(F32)
            qa = jnp.where(lane < DK, q32, 0.0).astype(BF16)
            qb = jnp.where(lane >= DK, q32, 0.0).astype(BF16)
            qs = jnp.concatenate([qa, qb], axis=0)
            sc = lax.dot_general(qs, kc, _NT, preferred_element_type=F32)
            pm = (sc * dmat_ref[pp]).astype(BF16)
            intra_a = jnp.dot(pm[0:c], vc[:, 0:DV], preferred_element_type=F32)
            intra_b = jnp.dot(pm[c:2 * c], vc[:, DV:2 * DV], preferred_element_type=F32)
            st = state_ref[pp]
            inter = jnp.dot(qc, st.astype(BF16), preferred_element_type=F32)
            out = jnp.concatenate([intra_a, intra_b], axis=1) + inter * qdec_ref[pp]

            kd_t = (kc.astype(F32) * kdec_ref[pp]).T.astype(BF16)
            upd = jnp.dot(kd_t, vc, preferred_element_type=F32)
            state_ref[pp] = st * sdec_ref[pp] + upd * smask_ref[...]

            g = g_ref[:, v_cols]
            act = gate_ref[rows, v_cols].astype(F32)
            halves = []
            for hh in range(2):
                o = out[:, hh * DV:(hh + 1) * DV]
                ms = jnp.mean(o * o, axis=-1, keepdims=True)
                halves.append(o * lax.rsqrt(ms + EPS) * g[:, hh * DV:(hh + 1) * DV])
            o_ref[rows, v_cols] = (jnp.concatenate(halves, axis=1) * act).astype(o_ref.dtype)
        return carry

    lax.fori_loop(0, RET_ROWS // c, chunk, 0)


def _retention(z3, log_gamma, g):
    b, seq, _ = z3.shape
    c = RET_CHUNK
    n_pairs = HEADS // 2
    qk_w = n_pairs * LANES
    v_w = HEADS * DV
    return pl.pallas_call(
        _retention_kernel,
        grid=(b, seq // RET_ROWS),
        in_specs=[
            pl.BlockSpec(memory_space=pltpu.SMEM),
            pl.BlockSpec((None, RET_ROWS, qk_w), lambda i, s: (i, s, COL_QR * LANES // qk_w)),
            pl.BlockSpec((None, RET_ROWS, qk_w), lambda i, s: (i, s, COL_KR * LANES // qk_w)),
            pl.BlockSpec((None, RET_ROWS, v_w), lambda i, s: (i, s, COL_VR * LANES // v_w)),
            pl.BlockSpec((None, RET_ROWS, v_w), lambda i, s: (i, s, COL_GR * LANES // v_w)),
            pl.BlockSpec((1, v_w), lambda i, s: (0, 0)),
        ],
        out_specs=pl.BlockSpec((None, RET_ROWS, v_w), lambda i, s: (i, s, 0)),
        out_shape=jax.ShapeDtypeStruct((b, seq, v_w), BF16),
        scratch_shapes=[
            pltpu.VMEM((n_pairs, 2 * c, c), F32),
            pltpu.VMEM((n_pairs, c, LANES), F32),
            pltpu.VMEM((n_pairs, c, 2 * DV), F32),
            pltpu.VMEM((n_pairs, 2 * DK, 2 * DV), F32),
            pltpu.VMEM((2 * DK, 2 * DV), F32),
            pltpu.VMEM((n_pairs, 2 * DK, 2 * DV), F32),
        ],
        compiler_params=pltpu.CompilerParams(dimension_semantics=("parallel", "arbitrary")),
        name="retention",
    )(log_gamma, z3, z3, z3, z3, g)


def _diffattn_kernel(slope_ref, qt_ref, k_ref, vt_ref, gate_ref, gq_ref, gk_ref,
                     lq1_ref, lk1_ref, lq2_ref, lk2_ref, gs_ref, o_ref,
                     kn_ref, qs_ref, alibi_ref, sa_ref, sb_ref, mx_ref, m_ref, acc_ref, *, lam_init):
    tq, tk = ATT_TQ, ATT_TK
    seq = k_ref.shape[0]
    sub = tq // tk
    slope = slope_ref[pl.program_id(1)]
    lam = (jnp.exp(jnp.sum(lq1_ref[...] * lk1_ref[...], axis=-1, keepdims=True))
           - jnp.exp(jnp.sum(lq2_ref[...] * lk2_ref[...], axis=-1, keepdims=True))
           + lam_init)
    same_map = ((lax.broadcasted_iota(jnp.int32, (LANES, LANES), 0) < DK)
                == (lax.broadcasted_iota(jnp.int32, (LANES, LANES), 1) < DK)).astype(F32).astype(BF16)

    def knorm(r, carry):
        rows = pl.ds(pl.multiple_of(r * NORM_ROWS, NORM_ROWS), NORM_ROWS)
        k32 = k_ref[rows, :].astype(F32)
        sq = k32 * k32
        hi = sq.astype(BF16)
        lo = (sq - hi.astype(F32)).astype(BF16)
        ss = (jnp.dot(hi, same_map, preferred_element_type=F32)
              + jnp.dot(lo, same_map, preferred_element_type=F32))
        kn_ref[rows, 0:LANES] = (k32 * lax.rsqrt(ss * (1.0 / DK) + EPS) * gk_ref[...]).astype(BF16)
        pos = r * NORM_ROWS + lax.broadcasted_iota(jnp.int32, (NORM_ROWS, LANES), 0)
        lane = lax.broadcasted_iota(jnp.int32, (NORM_ROWS, LANES), 1)
        lo = pos & (tk - 1)
        part = jnp.where(lane >= 2 * POS_COPIES, 0, jnp.where((lane & 1) == 0, lo, pos - lo))
        kn_ref[rows, LANES:2 * LANES] = part.astype(F32).astype(BF16)
        return carry
    lax.fori_loop(0, seq // NORM_ROWS, knorm, 0, unroll=8)

    c_full = jnp.full((LANES, 2 * tq), slope * LOG2E, F32)
    c0 = c_full.astype(BF16).astype(F32)
    c1 = (c_full - c0).astype(BF16).astype(F32)
    c2 = c_full - c0 - c1
    piece = lax.broadcasted_iota(jnp.int32, (LANES, 2 * tq), 0) >> 1
    alibi_ref[:, 0:2 * tq] = jnp.where(
        piece == 0, c0, jnp.where(piece == 1, c1, jnp.where(piece == 2, c2, 0.0))).astype(BF16)
    ones_rows = jnp.ones((SUM_ROWS, tk), BF16)
    n_q = seq // tq
    n_lanes = 2 * tq
    bufs = (sa_ref, sb_ref)

    def q_prep(qi, carry):
        qt = jnp.concatenate([qt_ref[sub * qi + c] for c in range(sub)], axis=1).astype(F32)
        gq = gq_ref[...]
        halves = []
        for hh in range(2):
            x = qt[hh * DK:(hh + 1) * DK]
            ms = jnp.mean(x * x, axis=0, keepdims=True)
            halves.append(x * lax.rsqrt(ms + EPS) * gq[hh * DK:(hh + 1) * DK] * (DK ** -0.5 * LOG2E))
        zeros = jnp.zeros((DK, tk), F32)
        q1, q2 = halves
        qs_ref[qi, 0:DK, 0:n_lanes] = jnp.concatenate(
            [piece for c in range(sub) for piece in (q1[:, c * tk:(c + 1) * tk], zeros)], axis=1).astype(BF16)
        qs_ref[qi, DK:2 * DK, 0:n_lanes] = jnp.concatenate(
            [piece for c in range(sub) for piece in (zeros, q2[:, c * tk:(c + 1) * tk])], axis=1).astype(BF16)
        return carry
    lax.fori_loop(0, n_q, q_prep, 0, unroll=2)

    def scores(j, b, qsel, start=0, with_max=True):
        keys = pl.ds(pl.multiple_of(j * tk, tk), tk)
        rhs = jnp.concatenate([qs_ref[qsel, :, start:n_lanes], alibi_ref[:, start:n_lanes]], axis=0)
        sv = jnp.dot(kn_ref[keys, 0:2 * LANES], rhs, preferred_element_type=F32)
        bufs[b][:, start:n_lanes] = sv
        if with_max:
            mx_ref[b] = jnp.max(sv, axis=0, keepdims=True)

    scores(0, 0, 0)

    def q_tile(qi, carry):
        m_ref[...] = jnp.full_like(m_ref, NEG)
        acc_ref[:, 0:n_lanes] = jnp.zeros((DV + SUM_ROWS, n_lanes), F32)

        def absorb(j, b, start=0, diag=False):
            src = bufs[b]
            lanes = slice(start, n_lanes)
            s = src[:, lanes]
            if diag:
                key = lax.broadcasted_iota(jnp.int32, (tk, 2 * tk), 0)
                qry = lax.broadcasted_iota(jnp.int32, (tk, 2 * tk), 1) & (tk - 1)
                masked = jnp.where(key <= qry, src[:, start:start + 2 * tk], NEG)
                rest = [src[:, start + 2 * tk:n_lanes]] if start + 2 * tk < n_lanes else []
                s = jnp.concatenate([masked] + rest, axis=1)
                m_tile = jnp.max(s, axis=0, keepdims=True)
            else:
                m_tile = mx_ref[b]
            m_old = m_ref[:, lanes]
            m_new = jnp.maximum(m_old, m_tile)
            alpha = jnp.exp2(m_old - m_new)
            p = jnp.exp2(s - m_new).astype(BF16)
            v_aug = jnp.concatenate([vt_ref[j], ones_rows], axis=0)
            acc_ref[:, lanes] = alpha * acc_ref[:, lanes] + jnp.dot(v_aug, p, preferred_element_type=F32)
            m_ref[:, lanes] = m_new

        def quad(mi, c2):
            for u in range(4):
                scores(4 * mi + u + 1, (u + 1) % 2, qi)
                absorb(4 * mi + u, u % 2)
            return c2
        lax.fori_loop(0, qi * (sub // 4), quad, 0)

        first_diag = sub * qi
        for d in range(sub):
            if d + 1 < sub:
                scores(first_diag + d + 1, (d + 1) % 2, qi, start=(d + 1) * 2 * tk, with_max=False)
            else:
                scores(0, 0, jnp.minimum(qi + 1, n_q - 1))
            absorb(first_diag + d, d % 2, start=d * 2 * tk, diag=True)

        inv = 1.0 / acc_ref[DV:DV + 1, 0:n_lanes]
        acc = acc_ref[0:DV, 0:n_lanes] * inv
        o1 = jnp.concatenate([acc[:, 2 * c * tk:(2 * c + 1) * tk] for c in range(sub)], axis=1)
        o2 = jnp.concatenate([acc[:, (2 * c + 1) * tk:(2 * c + 2) * tk] for c in range(sub)], axis=1)
        ot = o1 - lam * o2
        ms = jnp.mean(ot * ot, axis=0, keepdims=True)
        on = (ot * lax.rsqrt(ms + EPS)).T
        rows = pl.ds(pl.multiple_of(qi * tq, tq), tq)
        act = gate_ref[rows, :].astype(F32)
        o_ref[rows, :] = (on * gs_ref[...] * (1.0 - lam_init) * act).astype(o_ref.dtype)
        return carry

    lax.fori_loop(0, n_q, q_tile, 0)


def _diffattn(z3, t4, slopes, gq, gk, lq1, lk1, lq2, lk2, gs, lam_init):
    b, seq, _ = z3.shape
    tq, tk = ATT_TQ, ATT_TK
    assert tq == 4 * tk
    head_rows = lambda col: pl.BlockSpec((None, seq, LANES), lambda i, h: (i, 0, col + h))
    head_t = lambda off: pl.BlockSpec((None, seq // tk, LANES, tk), lambda i, h: (i, 0, off + h, 0))
    vec = lambda width: pl.BlockSpec((1, width), lambda i, h: (0, 0))
    return pl.pallas_call(
        functools.partial(_diffattn_kernel, lam_init=lam_init),
        grid=(b, HEADS),
        in_specs=[
            pl.BlockSpec(memory_space=pltpu.SMEM),
            head_t(0), head_rows(COL_KD), head_t(HEADS), head_rows(COL_GD),
            pl.BlockSpec((LANES, 1), lambda i, h: (0, 0)),
            vec(LANES), vec(DK), vec(DK), vec(DK), vec(DK),
            pl.BlockSpec((1, DV), lambda i, h: (0, h)),
        ],
        out_specs=pl.BlockSpec((None, seq, DV), lambda i, h: (i, 0, h)),
        out_shape=jax.ShapeDtypeStruct((b, seq, HEADS * DV), BF16),
        scratch_shapes=[
            pltpu.VMEM((seq, 3 * LANES), BF16),
            pltpu.VMEM((seq // tq, LANES, 2 * tq + LANES), BF16),
            pltpu.VMEM((LANES, 2 * tq + LANES), BF16),
            pltpu.VMEM((tk, 2 * tq + LANES), F32),
            pltpu.VMEM((tk, 2 * tq + LANES), F32),
            pltpu.VMEM((2, 1, 2 * tq), F32),
            pltpu.VMEM((1, 2 * tq), F32),
            pltpu.VMEM((DV + SUM_ROWS, 2 * tq + LANES), F32),
        ],
        compiler_params=pltpu.CompilerParams(dimension_semantics=("parallel", "parallel")),
        name="diffattn",
    )(slopes, t4, z3, t4, z3, gq, gk, lq1, lk1, lq2, lk2, gs)


def _outproj_kernel(x_ref, or_ref, od_ref, mr_ref, md_ref, wr_ref, wd_ref, wo_ref, out_ref):
    y_r = jnp.dot(or_ref[...], wr_ref[...], preferred_element_type=F32)
    y_d = jnp.dot(od_ref[...], wd_ref[...], preferred_element_type=F32)
    merged = mr_ref[...].astype(F32) * y_r + md_ref[...].astype(F32) * y_d
    out_ref[...] = x_ref[...] + jnp.dot(merged.astype(BF16), wo_ref[...], preferred_element_type=F32)


def _outproj(x2, o_r, o_d, z2, w_r, w_d, w_o, layer):
    m = x2.shape[0]
    rows = lambda col: pl.BlockSpec((OUT_TM, D_MODEL), lambda i: (i, col))
    weight = pl.BlockSpec((None, D_MODEL, D_MODEL), lambda i: (layer, 0, 0))
    return pl.pallas_call(
        _outproj_kernel,
        grid=(m // OUT_TM,),
        in_specs=[rows(0), rows(0), rows(0),
                  rows(COL_MR * LANES // D_MODEL), rows(COL_MD * LANES // D_MODEL),
                  weight, weight, weight],
        out_specs=rows(0),
        out_shape=jax.ShapeDtypeStruct((m, D_MODEL), F32),
        compiler_params=pltpu.CompilerParams(dimension_semantics=("parallel",)),
        name="outproj",
    )(x2, o_r, o_d, z2, z2, w_r, w_d, w_o)


def kernel(x, norm_g, w_in, ret_norm_g, ret_w_o, diff_q_norm_g, diff_k_norm_g,
           diff_lq1, diff_lk1, diff_lq2, diff_lk2, diff_sub_norm_g, diff_w_o, w_out):
    b, seq, d = x.shape
    depth = norm_g.shape[0]
    assert d == D_MODEL and w_in.shape[-1] == N_IN
    assert seq % ATT_TQ == 0 and seq % RET_ROWS == 0 and RET_ROWS % RET_CHUNK == 0 and seq % IN_TM == 0

    assert seq <= ATT_TK * 256
    slopes = jnp.exp2(-8.0 * jnp.arange(1, HEADS + 1, dtype=F32) / HEADS)
    log_gamma = jnp.log1p(-jnp.exp2(-5.0 - jnp.arange(HEADS, dtype=F32)))
    row = lambda v: v.reshape(1, -1).astype(F32)

    w_in_b, w_r_b, w_d_b, w_o_b = (w.astype(BF16) for w in (w_in, ret_w_o, diff_w_o, w_out))

    x2 = x.reshape(b * seq, d)
    for l in range(depth):
        lam_init = 0.8 - 0.6 * math.exp(-0.3 * l)
        z2, t4 = _inproj(x2, row(norm_g[l]), w_in_b, l, b, seq)
        z3 = z2.reshape(b, seq, N_MAIN)
        o_r = _retention(z3, log_gamma, row(ret_norm_g[l]))
        gq_col = jnp.concatenate([diff_q_norm_g[l], diff_q_norm_g[l]]).reshape(-1, 1).astype(F32)
        gk_row = jnp.concatenate([diff_k_norm_g[l], diff_k_norm_g[l]]).reshape(1, -1).astype(F32)
        o_d = _diffattn(z3, t4, slopes, gq_col, gk_row,
                        row(diff_lq1[l]), row(diff_lk1[l]), row(diff_lq2[l]), row(diff_lk2[l]),
                        row(diff_sub_norm_g[l]), lam_init)
        x2 = _outproj(x2, o_r.reshape(b * seq, -1), o_d.reshape(b * seq, -1), z2,
                      w_r_b, w_d_b, w_o_b, l)
    return x2.reshape(b, seq, d)
```

```python
import functools
import math

import jax
import jax.numpy as jnp
from jax import lax
from jax.experimental import pallas as pl
from jax.experimental.pallas import tpu as pltpu

D_MODEL = 1024
HEADS = 8
DK = 64
DV = 128
EPS = 1e-6
LANES = 128
N_IN = 9216
N_MAIN = 5120
N_T = 2048
N_H = 2048

COL_QR, COL_KR, COL_VR, COL_GR = 0, 4, 8, 16
COL_MR, COL_MD = 24, 32

IN_TM, IN_TN = 512, 1024
IN_VMEM_BYTES = (2 * D_MODEL * N_IN + 2 * IN_TM * D_MODEL * 4 + 2 * IN_TM * N_IN * 2
                 + IN_TM * D_MODEL * 2 + 2 * IN_TM * IN_TN * 4 + (1 << 20))
NORM_ROWS = 256
RET_CHUNK = 256
RET_ROWS = 1024
ATT_TQ = 1024
ATT_TK = 256
OUT_TM = 512
NEG = -1e30
LOG2E = 1.4426950408889634
POS_COPIES = 3
SUM_ROWS = 16

F32 = jnp.float32
BF16 = jnp.bfloat16
_NT = (((1,), (1,)), ((), ()))


def _silu(v):
    return v * jax.nn.sigmoid(v)


_CHUNK_PLAN = (
    ("z", None),
    ("z", None),
    ("z", _silu),
    ("t", 0),
    ("h", (0, None)),
    ("t", 1),
    ("h", (1, _silu)),
    ("z", jax.nn.sigmoid),
    ("z", jax.nn.sigmoid),
)


def _inproj_kernel(x_ref, g_ref, w_ref, z_ref, t_ref, hd_ref, h_ref):
    for r in range(IN_TM // NORM_ROWS):
        rows = slice(r * NORM_ROWS, (r + 1) * NORM_ROWS)
        x = x_ref[rows, :]
        ms = jnp.mean(x * x, axis=-1, keepdims=True)
        h_ref[rows, :] = (x * lax.rsqrt(ms + EPS) * g_ref[...]).astype(BF16)
    h = h_ref[...]
    z_chunk = 0
    for n, (kind, arg) in enumerate(_CHUNK_PLAN):
        y = jnp.dot(h, w_ref[:, n * IN_TN:(n + 1) * IN_TN], preferred_element_type=F32)
        if kind == "t":
            yt = y.T.astype(t_ref.dtype)
            for c in range(IN_TM // ATT_TK):
                t_ref[c, arg * IN_TN:(arg + 1) * IN_TN, :] = yt[:, c * ATT_TK:(c + 1) * ATT_TK]
        elif kind == "h":
            group, act = arg
            yb = (y if act is None else act(y)).astype(hd_ref.dtype)
            for hh in range(HEADS):
                hd_ref[group * HEADS + hh] = yb[:, hh * LANES:(hh + 1) * LANES]
        else:
            if arg is not None:
                y = arg(y)
            z_ref[:, z_chunk * IN_TN:(z_chunk + 1) * IN_TN] = y.astype(z_ref.dtype)
            z_chunk += 1


def _inproj(x2, g, w_all, layer, batch, seq):
    m = x2.shape[0]
    per_batch = seq // IN_TM
    sub = IN_TM // ATT_TK
    resident = dict(pipeline_mode=pl.Buffered(1))
    return pl.pallas_call(
        _inproj_kernel,
        grid=(m // IN_TM,),
        in_specs=[
            pl.BlockSpec((IN_TM, D_MODEL), lambda i: (i, 0)),
            pl.BlockSpec((1, D_MODEL), lambda i: (0, 0)),
            pl.BlockSpec((None, D_MODEL, N_IN), lambda i: (layer, 0, 0), **resident),
        ],
        out_specs=[
            pl.BlockSpec((IN_TM, N_MAIN), lambda i: (i, 0)),
            pl.BlockSpec((None, sub, N_T, ATT_TK), lambda i: (i // per_batch, i % per_batch, 0, 0)),
            pl.BlockSpec((None, N_H // LANES, IN_TM, LANES), lambda i: (i // per_batch, 0, i % per_batch, 0)),
        ],
        out_shape=[
            jax.ShapeDtypeStruct((m, N_MAIN), BF16),
            jax.ShapeDtypeStruct((batch, seq // ATT_TK, N_T, ATT_TK), BF16),
            jax.ShapeDtypeStruct((batch, N_H // LANES, seq, LANES), BF16),
        ],
        scratch_shapes=[pltpu.VMEM((IN_TM, D_MODEL), BF16)],
        compiler_params=pltpu.CompilerParams(dimension_semantics=("parallel",),
                                             vmem_limit_bytes=IN_VMEM_BYTES),
        name="inproj",
    )(x2, g, w_all)


def _retention_kernel(lg_ref, q_ref, k_ref, v_ref, gate_ref, g_ref, o_ref,
                      dmat_ref, kdec_ref, qdec_ref, sdec_ref, smask_ref, state_ref):
    c = RET_CHUNK
    n_pairs = HEADS // 2
    scale = DK ** -0.5

    @pl.when(pl.program_id(1) == 0)
    def _():
        n = lax.broadcasted_iota(jnp.int32, (c, c), 0)
        mm = lax.broadcasted_iota(jnp.int32, (c, c), 1)
        rel = (n - mm).astype(F32)
        causal = rel >= 0.0
        relc = jnp.maximum(rel, 0.0)
        pos_k = lax.broadcasted_iota(jnp.int32, (c, LANES), 0).astype(F32)
        lane_k = lax.broadcasted_iota(jnp.int32, (c, LANES), 1)
        pos_q = lax.broadcasted_iota(jnp.int32, (c, 2 * DV), 0).astype(F32)
        lane_q = lax.broadcasted_iota(jnp.int32, (c, 2 * DV), 1)
        srow = lax.broadcasted_iota(jnp.int32, (2 * DK, 2 * DV), 0)
        scol = lax.broadcasted_iota(jnp.int32, (2 * DK, 2 * DV), 1)
        in_a = (srow < DK) & (scol < DV)
        in_b = (srow >= DK) & (scol >= DV)
        smask_ref[...] = jnp.where(in_a | in_b, 1.0, 0.0)
        full_c = jnp.full((2 * DK, 2 * DV), c, F32)
        for pp in range(n_pairs):
            lga = lg_ref[2 * pp]
            lgb = lg_ref[2 * pp + 1]
            dmat_ref[pp, 0:c, :] = jnp.where(causal, jnp.exp(relc * lga), 0.0) * scale
            dmat_ref[pp, c:2 * c, :] = jnp.where(causal, jnp.exp(relc * lgb), 0.0) * scale
            kdec_ref[pp] = jnp.exp((c - 1.0 - pos_k) * jnp.where(lane_k < DK, lga, lgb)) * scale
            qdec_ref[pp] = jnp.exp((pos_q + 1.0) * jnp.where(lane_q < DV, lga, lgb))
            sdec_ref[pp] = (jnp.where(in_a, jnp.exp(full_c * lga), 0.0)
                            + jnp.where(in_b, jnp.exp(full_c * lgb), 0.0))
        state_ref[...] = jnp.zeros_like(state_ref)

    def chunk(ci, carry):
        rows = pl.ds(pl.multiple_of(ci * c, c), c)
        lane = lax.broadcasted_iota(jnp.int32, (c, LANES), 1)
        for pp in range(n_pairs):
            qk_cols = slice(pp * LANES, (pp + 1) * LANES)
            v_cols = slice(pp * 2 * DV, (pp + 1) * 2 * DV)
            qc = q_ref[rows, qk_cols]
            kc = k_ref[rows, qk_cols]
            vc = v_ref[rows, v_cols]
            q32 = qc.astype(F32)
            qa = jnp.where(lane < DK, q32, 0.0).astype(BF16)
            qb = jnp.where(lane >= DK, q32, 0.0).astype(BF16)
            qs = jnp.concatenate([qa, qb], axis=0)
            sc = lax.dot_general(qs, kc, _NT, preferred_element_type=F32)
            pm = (sc * dmat_ref[pp]).astype(BF16)
            intra_a = jnp.dot(pm[0:c], vc[:, 0:DV], preferred_element_type=F32)
            intra_b = jnp.dot(pm[c:2 * c], vc[:, DV:2 * DV], preferred_element_type=F32)
            st = state_ref[pp]
            inter = jnp.dot(qc, st.astype(BF16), preferred_element_type=F32)
            out = jnp.concatenate([intra_a, intra_b], axis=1) + inter * qdec_ref[pp]

            kd_t = (kc.astype(F32) * kdec_ref[pp]).T.astype(BF16)
            upd = jnp.dot(kd_t, vc, preferred_element_type=F32)
            state_ref[pp] = st * sdec_ref[pp] + upd * smask_ref[...]

            g = g_ref[:, v_cols]
            act = gate_ref[rows, v_cols].astype(F32)
            halves = []
            for hh in range(2):
                o = out[:, hh * DV:(hh + 1) * DV]
                ms = jnp.mean(o * o, axis=-1, keepdims=True)
                halves.append(o * lax.rsqrt(ms + EPS) * g[:, hh * DV:(hh + 1) * DV])
            o_ref[rows, v_cols] = (jnp.concatenate(halves, axis=1) * act).astype(o_ref.dtype)
        return carry

    lax.fori_loop(0, RET_ROWS // c, chunk, 0)


def _retention(z3, log_gamma, g):
    b, seq, _ = z3.shape
    c = RET_CHUNK
    n_pairs = HEADS // 2
    qk_w = n_pairs * LANES
    v_w = HEADS * DV
    return pl.pallas_call(
        _retention_kernel,
        grid=(b, seq // RET_ROWS),
        in_specs=[
            pl.BlockSpec(memory_space=pltpu.SMEM),
            pl.BlockSpec((None, RET_ROWS, qk_w), lambda i, s: (i, s, COL_QR * LANES // qk_w)),
            pl.BlockSpec((None, RET_ROWS, qk_w), lambda i, s: (i, s, COL_KR * LANES // qk_w)),
            pl.BlockSpec((None, RET_ROWS, v_w), lambda i, s: (i, s, COL_VR * LANES // v_w)),
            pl.BlockSpec((None, RET_ROWS, v_w), lambda i, s: (i, s, COL_GR * LANES // v_w)),
            pl.BlockSpec((1, v_w), lambda i, s: (0, 0)),
        ],
        out_specs=pl.BlockSpec((None, RET_ROWS, v_w), lambda i, s: (i, s, 0)),
        out_shape=jax.ShapeDtypeStruct((b, seq, v_w), BF16),
        scratch_shapes=[
            pltpu.VMEM((n_pairs, 2 * c, c), F32),
            pltpu.VMEM((n_pairs, c, LANES), F32),
            pltpu.VMEM((n_pairs, c, 2 * DV), F32),
            pltpu.VMEM((n_pairs, 2 * DK, 2 * DV), F32),
            pltpu.VMEM((2 * DK, 2 * DV), F32),
            pltpu.VMEM((n_pairs, 2 * DK, 2 * DV), F32),
        ],
        compiler_params=pltpu.CompilerParams(dimension_semantics=("parallel", "arbitrary")),
        name="retention",
    )(log_gamma, z3, z3, z3, z3, g)


def _diffattn_kernel(slope_ref, qt_ref, k_ref, vt_ref, gate_ref, gq_ref, gk_ref,
                     lq1_ref, lk1_ref, lq2_ref, lk2_ref, gs_ref, o_ref,
                     kn_ref, qs_ref, alibi_ref, sa_ref, sb_ref, mx_ref, m_ref, acc_ref, *, lam_init):
    tq, tk = ATT_TQ, ATT_TK
    seq = k_ref.shape[0]
    sub = tq // tk
    slope = slope_ref[pl.program_id(1)]
    lam = (jnp.exp(jnp.sum(lq1_ref[...] * lk1_ref[...], axis=-1, keepdims=True))
           - jnp.exp(jnp.sum(lq2_ref[...] * lk2_ref[...], axis=-1, keepdims=True))
           + lam_init)
    same_map = ((lax.broadcasted_iota(jnp.int32, (LANES, LANES), 0) < DK)
                == (lax.broadcasted_iota(jnp.int32, (LANES, LANES), 1) < DK)).astype(F32).astype(BF16)

    def knorm(r, carry):
        rows = pl.ds(pl.multiple_of(r * NORM_ROWS, NORM_ROWS), NORM_ROWS)
        k32 = k_ref[rows, :].astype(F32)
        sq = k32 * k32
        hi = sq.astype(BF16)
        lo = (sq - hi.astype(F32)).astype(BF16)
        ss = (jnp.dot(hi, same_map, preferred_element_type=F32)
              + jnp.dot(lo, same_map, preferred_element_type=F32))
        kn_ref[rows, 0:LANES] = (k32 * lax.rsqrt(ss * (1.0 / DK) + EPS) * gk_ref[...]).astype(BF16)
        pos = r * NORM_ROWS + lax.broadcasted_iota(jnp.int32, (NORM_ROWS, LANES), 0)
        lane = lax.broadcasted_iota(jnp.int32, (NORM_ROWS, LANES), 1)
        lo = pos & (tk - 1)
        part = jnp.where(lane >= 2 * POS_COPIES, 0, jnp.where((lane & 1) == 0, lo, pos - lo))
        kn_ref[rows, LANES:2 * LANES] = part.astype(F32).astype(BF16)
        return carry
    lax.fori_loop(0, seq // NORM_ROWS, knorm, 0, unroll=8)

    c_full = jnp.full((LANES, 2 * tq), slope * LOG2E, F32)
    c0 = c_full.astype(BF16).astype(F32)
    c1 = (c_full - c0).astype(BF16).astype(F32)
    c2 = c_full - c0 - c1
    piece = lax.broadcasted_iota(jnp.int32, (LANES, 2 * tq), 0) >> 1
    alibi_ref[:, 0:2 * tq] = jnp.where(
        piece == 0, c0, jnp.where(piece == 1, c1, jnp.where(piece == 2, c2, 0.0))).astype(BF16)
    ones_rows = jnp.ones((SUM_ROWS, tk), BF16)
    n_q = seq // tq
    n_lanes = 2 * tq
    bufs = (sa_ref, sb_ref)

    def q_prep(qi, carry):
        qt = jnp.concatenate([qt_ref[sub * qi + c] for c in range(sub)], axis=1).astype(F32)
        gq = gq_ref[...]
        halves = []
        for hh in range(2):
            x = qt[hh * DK:(hh + 1) * DK]
            ms = jnp.mean(x * x, axis=0, keepdims=True)
            halves.append(x * lax.rsqrt(ms + EPS) * gq[hh * DK:(hh + 1) * DK] * (DK ** -0.5 * LOG2E))
        zeros = jnp.zeros((DK, tk), F32)
        q1, q2 = halves
        qs_ref[qi, 0:DK, 0:n_lanes] = jnp.concatenate(
            [piece for c in range(sub) for piece in (q1[:, c * tk:(c + 1) * tk], zeros)], axis=1).astype(BF16)
        qs_ref[qi, DK:2 * DK, 0:n_lanes] = jnp.concatenate(
            [piece for c in range(sub) for piece in (zeros, q2[:, c * tk:(c + 1) * tk])], axis=1).astype(BF16)
        return carry
    lax.fori_loop(0, n_q, q_prep, 0, unroll=2)

    def scores(j, b, qsel, start=0, with_max=True):
        keys = pl.ds(pl.multiple_of(j * tk, tk), tk)
        rhs = jnp.concatenate([qs_ref[qsel, :, start:n_lanes], alibi_ref[:, start:n_lanes]], axis=0)
        sv = jnp.dot(kn_ref[keys, :], rhs, preferred_element_type=F32)
        bufs[b][:, start:n_lanes] = sv
        if with_max:
            mx_ref[b] = jnp.max(sv, axis=0, keepdims=True)

    scores(0, 0, 0)

    def q_tile(qi, carry):
        m_ref[...] = jnp.full_like(m_ref, NEG)
        acc_ref[:, 0:n_lanes] = jnp.zeros((DV + SUM_ROWS, n_lanes), F32)

        def absorb(j, b, start=0, diag=False):
            src = bufs[b]
            lanes = slice(start, n_lanes)
            s = src[:, lanes]
            if diag:
                key = lax.broadcasted_iota(jnp.int32, (tk, 2 * tk), 0)
                qry = lax.broadcasted_iota(jnp.int32, (tk, 2 * tk), 1) & (tk - 1)
                masked = jnp.where(key <= qry, src[:, start:start + 2 * tk], NEG)
                rest = [src[:, start + 2 * tk:n_lanes]] if start + 2 * tk < n_lanes else []
                s = jnp.concatenate([masked] + rest, axis=1)
                m_tile = jnp.max(s, axis=0, keepdims=True)
            else:
                m_tile = mx_ref[b]
            m_old = m_ref[:, lanes]
            m_new = jnp.maximum(m_old, m_tile)
            alpha = jnp.exp2(m_old - m_new)
            p = jnp.exp2(s - m_new).astype(BF16)
            v_aug = jnp.concatenate([vt_ref[j], ones_rows], axis=0)
            acc_ref[:, lanes] = alpha * acc_ref[:, lanes] + jnp.dot(v_aug, p, preferred_element_type=F32)
            m_ref[:, lanes] = m_new

        def quad(mi, c2):
            for u in range(4):
                scores(4 * mi + u + 1, (u + 1) % 2, qi)
                absorb(4 * mi + u, u % 2)
            return c2
        lax.fori_loop(0, qi * (sub // 4), quad, 0)

        first_diag = sub * qi
        for d in range(sub):
            if d + 1 < sub:
                scores(first_diag + d + 1, (d + 1) % 2, qi, start=(d + 1) * 2 * tk, with_max=False)
            else:
                scores(0, 0, jnp.minimum(qi + 1, n_q - 1))
            absorb(first_diag + d, d % 2, start=d * 2 * tk, diag=True)

        inv = 1.0 / acc_ref[DV:DV + 1, 0:n_lanes]
        acc = acc_ref[0:DV, 0:n_lanes] * inv
        o1 = jnp.concatenate([acc[:, 2 * c * tk:(2 * c + 1) * tk] for c in range(sub)], axis=1)
        o2 = jnp.concatenate([acc[:, (2 * c + 1) * tk:(2 * c + 2) * tk] for c in range(sub)], axis=1)
        ot = o1 - lam * o2
        ms = jnp.mean(ot * ot, axis=0, keepdims=True)
        on = (ot * lax.rsqrt(ms + EPS)).T
        rows = pl.ds(pl.multiple_of(qi * tq, tq), tq)
        act = gate_ref[rows, :].astype(F32)
        o_ref[rows, :] = (on * gs_ref[...] * (1.0 - lam_init) * act).astype(o_ref.dtype)
        return carry

    lax.fori_loop(0, n_q, q_tile, 0)


def _diffattn(hd, t4, slopes, gq, gk, lq1, lk1, lq2, lk2, gs, lam_init):
    b, _, seq, _ = hd.shape
    tq, tk = ATT_TQ, ATT_TK
    assert tq == 4 * tk
    head_rows = lambda off: pl.BlockSpec((None, None, seq, LANES), lambda i, h: (i, off + h, 0, 0))
    head_t = lambda off: pl.BlockSpec((None, seq // tk, LANES, tk), lambda i, h: (i, 0, off + h, 0))
    vec = lambda width: pl.BlockSpec((1, width), lambda i, h: (0, 0))
    return pl.pallas_call(
        functools.partial(_diffattn_kernel, lam_init=lam_init),
        grid=(b, HEADS),
        in_specs=[
            pl.BlockSpec(memory_space=pltpu.SMEM),
            head_t(0), head_rows(0), head_t(HEADS), head_rows(HEADS),
            pl.BlockSpec((LANES, 1), lambda i, h: (0, 0)),
            vec(LANES), vec(DK), vec(DK), vec(DK), vec(DK),
            pl.BlockSpec((1, DV), lambda i, h: (0, h)),
        ],
        out_specs=pl.BlockSpec((None, None, seq, DV), lambda i, h: (i, h, 0, 0)),
        out_shape=jax.ShapeDtypeStruct((b, HEADS, seq, DV), BF16),
        scratch_shapes=[
            pltpu.VMEM((seq, 2 * LANES), BF16),
            pltpu.VMEM((seq // tq, LANES, 2 * tq + LANES), BF16),
            pltpu.VMEM((LANES, 2 * tq + LANES), BF16),
            pltpu.VMEM((tk, 2 * tq + LANES), F32),
            pltpu.VMEM((tk, 2 * tq + LANES), F32),
            pltpu.VMEM((2, 1, 2 * tq), F32),
            pltpu.VMEM((1, 2 * tq), F32),
            pltpu.VMEM((DV + SUM_ROWS, 2 * tq + LANES), F32),
        ],
        compiler_params=pltpu.CompilerParams(dimension_semantics=("parallel", "parallel")),
        name="diffattn",
    )(slopes, t4, hd, t4, hd, gq, gk, lq1, lk1, lq2, lk2, gs)


def _outproj_kernel(x_ref, or_ref, od_ref, mr_ref, md_ref, wr_ref, wd_ref, wo_ref, out_ref):
    y_r = jnp.dot(or_ref[...], wr_ref[...], preferred_element_type=F32)
    o_d = jnp.concatenate([od_ref[hh] for hh in range(HEADS)], axis=1)
    y_d = jnp.dot(o_d, wd_ref[...], preferred_element_type=F32)
    merged = mr_ref[...].astype(F32) * y_r + md_ref[...].astype(F32) * y_d
    out_ref[...] = x_ref[...] + jnp.dot(merged.astype(BF16), wo_ref[...], preferred_element_type=F32)


def _outproj(x2, o_r, o_d, z2, w_r, w_d, w_o, layer):
    m = x2.shape[0]
    per_batch = o_d.shape[2] // OUT_TM
    rows = lambda col: pl.BlockSpec((OUT_TM, D_MODEL), lambda i: (i, col))
    head_major = pl.BlockSpec((None, HEADS, OUT_TM, DV), lambda i: (i // per_batch, 0, i % per_batch, 0))
    weight = pl.BlockSpec((None, D_MODEL, D_MODEL), lambda i: (layer, 0, 0))
    return pl.pallas_call(
        _outproj_kernel,
        grid=(m // OUT_TM,),
        in_specs=[rows(0), rows(0), head_major,
                  rows(COL_MR * LANES // D_MODEL), rows(COL_MD * LANES // D_MODEL),
                  weight, weight, weight],
        out_specs=rows(0),
        out_shape=jax.ShapeDtypeStruct((m, D_MODEL), F32),
        compiler_params=pltpu.CompilerParams(dimension_semantics=("parallel",)),
        name="outproj",
    )(x2, o_r, o_d, z2, z2, w_r, w_d, w_o)


def kernel(x, norm_g, w_in, ret_norm_g, ret_w_o, diff_q_norm_g, diff_k_norm_g,
           diff_lq1, diff_lk1, diff_lq2, diff_lk2, diff_sub_norm_g, diff_w_o, w_out):
    b, seq, d = x.shape
    depth = norm_g.shape[0]
    assert d == D_MODEL and w_in.shape[-1] == N_IN
    assert seq % ATT_TQ == 0 and seq % RET_ROWS == 0 and RET_ROWS % RET_CHUNK == 0 and seq % IN_TM == 0

    assert seq <= ATT_TK * 256
    slopes = jnp.exp2(-8.0 * jnp.arange(1, HEADS + 1, dtype=F32) / HEADS)
    log_gamma = jnp.log1p(-jnp.exp2(-5.0 - jnp.arange(HEADS, dtype=F32)))
    row = lambda v: v.reshape(1, -1).astype(F32)

    w_in_b, w_r_b, w_d_b, w_o_b = (w.astype(BF16) for w in (w_in, ret_w_o, diff_w_o, w_out))

    x2 = x.reshape(b * seq, d)
    for l in range(depth):
        lam_init = 0.8 - 0.6 * math.exp(-0.3 * l)
        z2, t4, hd = _inproj(x2, row(norm_g[l]), w_in_b, l, b, seq)
        z3 = z2.reshape(b, seq, N_MAIN)
        o_r = _retention(z3, log_gamma, row(ret_norm_g[l]))
        gq_col = jnp.concatenate([diff_q_norm_g[l], diff_q_norm_g[l]]).reshape(-1, 1).astype(F32)
        gk_row = jnp.concatenate([diff_k_norm_g[l], diff_k_norm_g[l]]).reshape(1, -1).astype(F32)
        o_d = _diffattn(hd, t4, slopes, gq_col, gk_row,
                        row(diff_lq1[l]), row(diff_lk1[l]), row(diff_lq2[l]), row(diff_lk2[l]),
                        row(diff_sub_norm_g[l]), lam_init)
        x2 = _outproj(x2, o_r.reshape(b * seq, -1), o_d, z2, w_r_b, w_d_b, w_o_b, l)
    return x2.reshape(b, seq, d)
```

```python
import functools
import math

import jax
import jax.numpy as jnp
from jax import lax
from jax.experimental import pallas as pl
from jax.experimental.pallas import tpu as pltpu

D_MODEL = 1024
HEADS = 8
DK = 64
DV = 128
EPS = 1e-6
LANES = 128
N_IN = 9216
N_MAIN = 5120
N_T = 2048
N_H = 2048

COL_QR, COL_KR, COL_VR, COL_GR = 0, 4, 8, 16
COL_MR, COL_MD = 24, 32

IN_TM, IN_TN = 512, 1024
IN_VMEM_BYTES = (2 * D_MODEL * N_IN + 2 * IN_TM * D_MODEL * 4 + 2 * IN_TM * N_IN * 2
                 + IN_TM * D_MODEL * 2 + 2 * IN_TM * IN_TN * 4 + (1 << 20))
NORM_ROWS = 256
RET_CHUNK = 256
RET_ROWS = 1024
ATT_TQ = 1024
ATT_TK = 256
OUT_TM = 512
NEG = -1e30
LOG2E = 1.4426950408889634
POS_COPIES = 3
SUM_ROWS = 16

F32 = jnp.float32
BF16 = jnp.bfloat16
_NT = (((1,), (1,)), ((), ()))


def _silu(v):
    return v * jax.nn.sigmoid(v)


_CHUNK_PLAN = (
    ("z", None),
    ("z", None),
    ("z", _silu),
    ("t", 0),
    ("h", (0, None)),
    ("t", 1),
    ("h", (1, _silu)),
    ("z", jax.nn.sigmoid),
    ("z", jax.nn.sigmoid),
)


def _inproj_kernel(x_ref, g_ref, w_ref, z_ref, t_ref, hd_ref, h_ref):
    for r in range(IN_TM // NORM_ROWS):
        rows = slice(r * NORM_ROWS, (r + 1) * NORM_ROWS)
        x = x_ref[rows, :]
        ms = jnp.mean(x * x, axis=-1, keepdims=True)
        h_ref[rows, :] = (x * lax.rsqrt(ms + EPS) * g_ref[...]).astype(BF16)
    z_chunk = 0
    for n, (kind, arg) in enumerate(_CHUNK_PLAN):
        for c in range(IN_TM // ATT_TK):
            rows = slice(c * ATT_TK, (c + 1) * ATT_TK)
            y = jnp.dot(h_ref[rows, :], w_ref[:, n * IN_TN:(n + 1) * IN_TN], preferred_element_type=F32)
            if kind == "t":
                t_ref[c, arg * IN_TN:(arg + 1) * IN_TN, :] = y.T.astype(t_ref.dtype)
            elif kind == "h":
                group, act = arg
                yb = (y if act is None else act(y)).astype(hd_ref.dtype)
                for hh in range(HEADS):
                    hd_ref[group * HEADS + hh, rows, :] = yb[:, hh * LANES:(hh + 1) * LANES]
            else:
                if arg is not None:
                    y = arg(y)
                z_ref[rows, z_chunk * IN_TN:(z_chunk + 1) * IN_TN] = y.astype(z_ref.dtype)
        if kind == "z":
            z_chunk += 1


def _inproj(x2, g, w_all, layer, batch, seq):
    m = x2.shape[0]
    per_batch = seq // IN_TM
    sub = IN_TM // ATT_TK
    resident = dict(pipeline_mode=pl.Buffered(1))
    return pl.pallas_call(
        _inproj_kernel,
        grid=(m // IN_TM,),
        in_specs=[
            pl.BlockSpec((IN_TM, D_MODEL), lambda i: (i, 0)),
            pl.BlockSpec((1, D_MODEL), lambda i: (0, 0)),
            pl.BlockSpec((None, D_MODEL, N_IN), lambda i: (layer, 0, 0), **resident),
        ],
        out_specs=[
            pl.BlockSpec((IN_TM, N_MAIN), lambda i: (i, 0)),
            pl.BlockSpec((None, sub, N_T, ATT_TK), lambda i: (i // per_batch, i % per_batch, 0, 0)),
            pl.BlockSpec((None, N_H // LANES, IN_TM, LANES), lambda i: (i // per_batch, 0, i % per_batch, 0)),
        ],
        out_shape=[
            jax.ShapeDtypeStruct((m, N_MAIN), BF16),
            jax.ShapeDtypeStruct((batch, seq // ATT_TK, N_T, ATT_TK), BF16),
            jax.ShapeDtypeStruct((batch, N_H // LANES, seq, LANES), BF16),
        ],
        scratch_shapes=[pltpu.VMEM((IN_TM, D_MODEL), BF16)],
        compiler_params=pltpu.CompilerParams(dimension_semantics=("parallel",),
                                             vmem_limit_bytes=IN_VMEM_BYTES),
        name="inproj",
    )(x2, g, w_all)


def _retention_kernel(lg_ref, q_ref, k_ref, v_ref, gate_ref, g_ref, o_ref,
                      dmat_ref, kdec_ref, qdec_ref, sdec_ref, smask_ref, state_ref):
    c = RET_CHUNK
    n_pairs = HEADS // 2
    scale = DK ** -0.5

    @pl.when(pl.program_id(1) == 0)
    def _():
        n = lax.broadcasted_iota(jnp.int32, (c, c), 0)
        mm = lax.broadcasted_iota(jnp.int32, (c, c), 1)
        rel = (n - mm).astype(F32)
        causal = rel >= 0.0
        relc = jnp.maximum(rel, 0.0)
        pos_k = lax.broadcasted_iota(jnp.int32, (c, LANES), 0).astype(F32)
        lane_k = lax.broadcasted_iota(jnp.int32, (c, LANES), 1)
        pos_q = lax.broadcasted_iota(jnp.int32, (c, 2 * DV), 0).astype(F32)
        lane_q = lax.broadcasted_iota(jnp.int32, (c, 2 * DV), 1)
        srow = lax.broadcasted_iota(jnp.int32, (2 * DK, 2 * DV), 0)
        scol = lax.broadcasted_iota(jnp.int32, (2 * DK, 2 * DV), 1)
        in_a = (srow < DK) & (scol < DV)
        in_b = (srow >= DK) & (scol >= DV)
        smask_ref[...] = jnp.where(in_a | in_b, 1.0, 0.0)
        full_c = jnp.full((2 * DK, 2 * DV), c, F32)
        for pp in range(n_pairs):
            lga = lg_ref[2 * pp]
            lgb = lg_ref[2 * pp + 1]
            dmat_ref[pp, 0:c, :] = jnp.where(causal, jnp.exp(relc * lga), 0.0) * scale
            dmat_ref[pp, c:2 * c, :] = jnp.where(causal, jnp.exp(relc * lgb), 0.0) * scale
            kdec_ref[pp] = jnp.exp((c - 1.0 - pos_k) * jnp.where(lane_k < DK, lga, lgb)) * scale
            qdec_ref[pp] = jnp.exp((pos_q + 1.0) * jnp.where(lane_q < DV, lga, lgb))
            sdec_ref[pp] = (jnp.where(in_a, jnp.exp(full_c * lga), 0.0)
                            + jnp.where(in_b, jnp.exp(full_c * lgb), 0.0))
        state_ref[...] = jnp.zeros_like(state_ref)

    def chunk(ci, carry):
        rows = pl.ds(pl.multiple_of(ci * c, c), c)
        lane = lax.broadcasted_iota(jnp.int32, (c, LANES), 1)
        for pp in range(n_pairs):
            qk_cols = slice(pp * LANES, (pp + 1) * LANES)
            v_cols = slice(pp * 2 * DV, (pp + 1) * 2 * DV)
            qc = q_ref[rows, qk_cols]
            kc = k_ref[rows, qk_cols]
            vc = v_ref[rows, v_cols]
            q32 = qc.astype(F32)
            qa = jnp.where(lane < DK, q32, 0.0).astype(BF16)
            qb = jnp.where(lane >= DK, q32, 0.0).astype(BF16)
            qs = jnp.concatenate([qa, qb], axis=0)
            sc = lax.dot_general(qs, kc, _NT, preferred_element_type=F32)
            pm = (sc * dmat_ref[pp]).astype(BF16)
            intra_a = jnp.dot(pm[0:c], vc[:, 0:DV], preferred_element_type=F32)
            intra_b = jnp.dot(pm[c:2 * c], vc[:, DV:2 * DV], preferred_element_type=F32)
            st = state_ref[pp]
            inter = jnp.dot(qc, st.astype(BF16), preferred_element_type=F32)
            out = jnp.concatenate([intra_a, intra_b], axis=1) + inter * qdec_ref[pp]

            kd_t = (kc.astype(F32) * kdec_ref[pp]).T.astype(BF16)
            upd = jnp.dot(kd_t, vc, preferred_element_type=F32)
            state_ref[pp] = st * sdec_ref[pp] + upd * smask_ref[...]

            g = g_ref[:, v_cols]
            act = gate_ref[rows, v_cols].astype(F32)
            halves = []
            for hh in range(2):
                o = out[:, hh * DV:(hh + 1) * DV]
                ms = jnp.mean(o * o, axis=-1, keepdims=True)
                halves.append(o * lax.rsqrt(ms + EPS) * g[:, hh * DV:(hh + 1) * DV])
            o_ref[rows, v_cols] = (jnp.concatenate(halves, axis=1) * act).astype(o_ref.dtype)
        return carry

    lax.fori_loop(0, RET_ROWS // c, chunk, 0)


def _retention(z3, log_gamma, g):
    b, seq, _ = z3.shape
    c = RET_CHUNK
    n_pairs = HEADS // 2
    qk_w = n_pairs * LANES
    v_w = HEADS * DV
    return pl.pallas_call(
        _retention_kernel,
        grid=(b, seq // RET_ROWS),
        in_specs=[
            pl.BlockSpec(memory_space=pltpu.SMEM),
            pl.BlockSpec((None, RET_ROWS, qk_w), lambda i, s: (i, s, COL_QR * LANES // qk_w)),
            pl.BlockSpec((None, RET_ROWS, qk_w), lambda i, s: (i, s, COL_KR * LANES // qk_w)),
            pl.BlockSpec((None, RET_ROWS, v_w), lambda i, s: (i, s, COL_VR * LANES // v_w)),
            pl.BlockSpec((None, RET_ROWS, v_w), lambda i, s: (i, s, COL_GR * LANES // v_w)),
            pl.BlockSpec((1, v_w), lambda i, s: (0, 0)),
        ],
        out_specs=pl.BlockSpec((None, RET_ROWS, v_w), lambda i, s: (i, s, 0)),
        out_shape=jax.ShapeDtypeStruct((b, seq, v_w), BF16),
        scratch_shapes=[
            pltpu.VMEM((n_pairs, 2 * c, c), F32),
            pltpu.VMEM((n_pairs, c, LANES), F32),
            pltpu.VMEM((n_pairs, c, 2 * DV), F32),
            pltpu.VMEM((n_pairs, 2 * DK, 2 * DV), F32),
            pltpu.VMEM((2 * DK, 2 * DV), F32),
            pltpu.VMEM((n_pairs, 2 * DK, 2 * DV), F32),
        ],
        compiler_params=pltpu.CompilerParams(dimension_semantics=("parallel", "arbitrary")),
        name="retention",
    )(log_gamma, z3, z3, z3, z3, g)


def _diffattn_kernel(slope_ref, qt_ref, k_ref, vt_ref, gate_ref, gq_ref, gk_ref,
                     lq1_ref, lk1_ref, lq2_ref, lk2_ref, gs_ref, o_ref,
                     kn_ref, qs_ref, alibi_ref, sa_ref, sb_ref, mx_ref, m_ref, acc_ref, *, lam_init):
    tq, tk = ATT_TQ, ATT_TK
    seq = k_ref.shape[0]
    sub = tq // tk
    slope = slope_ref[pl.program_id(1)]
    lam = (jnp.exp(jnp.sum(lq1_ref[...] * lk1_ref[...], axis=-1, keepdims=True))
           - jnp.exp(jnp.sum(lq2_ref[...] * lk2_ref[...], axis=-1, keepdims=True))
           + lam_init)
    same_map = ((lax.broadcasted_iota(jnp.int32, (LANES, LANES), 0) < DK)
                == (lax.broadcasted_iota(jnp.int32, (LANES, LANES), 1) < DK)).astype(F32).astype(BF16)

    def knorm(r, carry):
        rows = pl.ds(pl.multiple_of(r * NORM_ROWS, NORM_ROWS), NORM_ROWS)
        k32 = k_ref[rows, :].astype(F32)
        sq = k32 * k32
        hi = sq.astype(BF16)
        lo = (sq - hi.astype(F32)).astype(BF16)
        ss = (jnp.dot(hi, same_map, preferred_element_type=F32)
              + jnp.dot(lo, same_map, preferred_element_type=F32))
        kn_ref[rows, 0:LANES] = (k32 * lax.rsqrt(ss * (1.0 / DK) + EPS) * gk_ref[...]).astype(BF16)
        pos = r * NORM_ROWS + lax.broadcasted_iota(jnp.int32, (NORM_ROWS, LANES), 0)
        lane = lax.broadcasted_iota(jnp.int32, (NORM_ROWS, LANES), 1)
        lo = pos & (tk - 1)
        part = jnp.where(lane >= 2 * POS_COPIES, 0, jnp.where((lane & 1) == 0, lo, pos - lo))
        kn_ref[rows, LANES:2 * LANES] = part.astype(F32).astype(BF16)
        return carry
    lax.fori_loop(0, seq // NORM_ROWS, knorm, 0, unroll=8)

    c_full = jnp.full((LANES, 2 * tq), slope * LOG2E, F32)
    c0 = c_full.astype(BF16).astype(F32)
    c1 = (c_full - c0).astype(BF16).astype(F32)
    c2 = c_full - c0 - c1
    piece = lax.broadcasted_iota(jnp.int32, (LANES, 2 * tq), 0) >> 1
    alibi_ref[:, 0:2 * tq] = jnp.where(
        piece == 0, c0, jnp.where(piece == 1, c1, jnp.where(piece == 2, c2, 0.0))).astype(BF16)
    ones_rows = jnp.ones((SUM_ROWS, tk), BF16)
    n_q = seq // tq
    n_lanes = 2 * tq
    bufs = (sa_ref, sb_ref)

    def q_prep(qi, carry):
        qt = jnp.concatenate([qt_ref[sub * qi + c] for c in range(sub)], axis=1).astype(F32)
        gq = gq_ref[...]
        halves = []
        for hh in range(2):
            x = qt[hh * DK:(hh + 1) * DK]
            ms = jnp.mean(x * x, axis=0, keepdims=True)
            halves.append(x * lax.rsqrt(ms + EPS) * gq[hh * DK:(hh + 1) * DK] * (DK ** -0.5 * LOG2E))
        zeros = jnp.zeros((DK, tk), F32)
        q1, q2 = halves
        qs_ref[qi, 0:DK, 0:n_lanes] = jnp.concatenate(
            [piece for c in range(sub) for piece in (q1[:, c * tk:(c + 1) * tk], zeros)], axis=1).astype(BF16)
        qs_ref[qi, DK:2 * DK, 0:n_lanes] = jnp.concatenate(
            [piece for c in range(sub) for piece in (zeros, q2[:, c * tk:(c + 1) * tk])], axis=1).astype(BF16)
        return carry
    lax.fori_loop(0, n_q, q_prep, 0, unroll=2)

    def scores(j, b, qsel, start=0, with_max=True):
        keys = pl.ds(pl.multiple_of(j * tk, tk), tk)
        rhs = jnp.concatenate([qs_ref[qsel, :, start:n_lanes], alibi_ref[:, start:n_lanes]], axis=0)
        sv = jnp.dot(kn_ref[keys, :], rhs, preferred_element_type=F32)
        bufs[b][:, start:n_lanes] = sv
        if with_max:
            mx_ref[b] = jnp.max(sv, axis=0, keepdims=True)

    scores(0, 0, 0)

    def q_tile(qi, carry):
        m_ref[...] = jnp.full_like(m_ref, NEG)
        acc_ref[:, 0:n_lanes] = jnp.zeros((DV + SUM_ROWS, n_lanes), F32)

        def absorb(j, b, start=0, diag=False):
            src = bufs[b]
            lanes = slice(start, n_lanes)
            s = src[:, lanes]
            if diag:
                key = lax.broadcasted_iota(jnp.int32, (tk, 2 * tk), 0)
                qry = lax.broadcasted_iota(jnp.int32, (tk, 2 * tk), 1) & (tk - 1)
                masked = jnp.where(key <= qry, src[:, start:start + 2 * tk], NEG)
                rest = [src[:, start + 2 * tk:n_lanes]] if start + 2 * tk < n_lanes else []
                s = jnp.concatenate([masked] + rest, axis=1)
                m_tile = jnp.max(s, axis=0, keepdims=True)
            else:
                m_tile = mx_ref[b]
            m_old = m_ref[:, lanes]
            m_new = jnp.maximum(m_old, m_tile)
            alpha = jnp.exp2(m_old - m_new)
            p = jnp.exp2(s - m_new).astype(BF16)
            v_aug = jnp.concatenate([vt_ref[j], ones_rows], axis=0)
            acc_ref[:, lanes] = alpha * acc_ref[:, lanes] + jnp.dot(v_aug, p, preferred_element_type=F32)
            m_ref[:, lanes] = m_new

        def quad(mi, c2):
            for u in range(4):
                scores(4 * mi + u + 1, (u + 1) % 2, qi)
                absorb(4 * mi + u, u % 2)
            return c2
        lax.fori_loop(0, qi * (sub // 4), quad, 0)

        first_diag = sub * qi
        for d in range(sub):
            if d + 1 < sub:
                scores(first_diag + d + 1, (d + 1) % 2, qi, start=(d + 1) * 2 * tk, with_max=False)
            else:
                scores(0, 0, jnp.minimum(qi + 1, n_q - 1))
            absorb(first_diag + d, d % 2, start=d * 2 * tk, diag=True)

        inv = 1.0 / acc_ref[DV:DV + 1, 0:n_lanes]
        acc = acc_ref[0:DV, 0:n_lanes] * inv
        o1 = jnp.concatenate([acc[:, 2 * c * tk:(2 * c + 1) * tk] for c in range(sub)], axis=1)
        o2 = jnp.concatenate([acc[:, (2 * c + 1) * tk:(2 * c + 2) * tk] for c in range(sub)], axis=1)
        ot = o1 - lam * o2
        ms = jnp.mean(ot * ot, axis=0, keepdims=True)
        on = (ot * lax.rsqrt(ms + EPS)).T
        rows = pl.ds(pl.multiple_of(qi * tq, tq), tq)
        act = gate_ref[rows, :].astype(F32)
        o_ref[rows, :] = (on * gs_ref[...] * (1.0 - lam_init) * act).astype(o_ref.dtype)
        return carry

    lax.fori_loop(0, n_q, q_tile, 0)


def _diffattn(hd, t4, slopes, gq, gk, lq1, lk1, lq2, lk2, gs, lam_init):
    b, _, seq, _ = hd.shape
    tq, tk = ATT_TQ, ATT_TK
    assert tq == 4 * tk
    head_rows = lambda off: pl.BlockSpec((None, None, seq, LANES), lambda i, h: (i, off + h, 0, 0))
    head_t = lambda off: pl.BlockSpec((None, seq // tk, LANES, tk), lambda i, h: (i, 0, off + h, 0))
    vec = lambda width: pl.BlockSpec((1, width), lambda i, h: (0, 0))
    return pl.pallas_call(
        functools.partial(_diffattn_kernel, lam_init=lam_init),
        grid=(b, HEADS),
        in_specs=[
            pl.BlockSpec(memory_space=pltpu.SMEM),
            head_t(0), head_rows(0), head_t(HEADS), head_rows(HEADS),
            pl.BlockSpec((LANES, 1), lambda i, h: (0, 0)),
            vec(LANES), vec(DK), vec(DK), vec(DK), vec(DK),
            pl.BlockSpec((1, DV), lambda i, h: (0, h)),
        ],
        out_specs=pl.BlockSpec((None, None, seq, DV), lambda i, h: (i, h, 0, 0)),
        out_shape=jax.ShapeDtypeStruct((b, HEADS, seq, DV), BF16),
        scratch_shapes=[
            pltpu.VMEM((seq, 2 * LANES), BF16),
            pltpu.VMEM((seq // tq, LANES, 2 * tq + LANES), BF16),
            pltpu.VMEM((LANES, 2 * tq + LANES), BF16),
            pltpu.VMEM((tk, 2 * tq + LANES), F32),
            pltpu.VMEM((tk, 2 * tq + LANES), F32),
            pltpu.VMEM((2, 1, 2 * tq), F32),
            pltpu.VMEM((1, 2 * tq), F32),
            pltpu.VMEM((DV + SUM_ROWS, 2 * tq + LANES), F32),
        ],
        compiler_params=pltpu.CompilerParams(dimension_semantics=("parallel", "parallel")),
        name="diffattn",
    )(slopes, t4, hd, t4, hd, gq, gk, lq1, lk1, lq2, lk2, gs)


def _outproj_kernel(x_ref, or_ref, od_ref, mr_ref, md_ref, wr_ref, wd_ref, wo_ref, out_ref):
    y_r = jnp.dot(or_ref[...], wr_ref[...], preferred_element_type=F32)
    o_d = jnp.concatenate([od_ref[hh] for hh in range(HEADS)], axis=1)
    y_d = jnp.dot(o_d, wd_ref[...], preferred_element_type=F32)
    merged = mr_ref[...].astype(F32) * y_r + md_ref[...].astype(F32) * y_d
    out_ref[...] = x_ref[...] + jnp.dot(merged.astype(BF16), wo_ref[...], preferred_element_type=F32)


def _outproj(x2, o_r, o_d, z2, w_r, w_d, w_o, layer):
    m = x2.shape[0]
    per_batch = o_d.shape[2] // OUT_TM
    rows = lambda col: pl.BlockSpec((OUT_TM, D_MODEL), lambda i: (i, col))
    head_major = pl.BlockSpec((None, HEADS, OUT_TM, DV), lambda i: (i // per_batch, 0, i % per_batch, 0))
    weight = pl.BlockSpec((None, D_MODEL, D_MODEL), lambda i: (layer, 0, 0))
    return pl.pallas_call(
        _outproj_kernel,
        grid=(m // OUT_TM,),
        in_specs=[rows(0), rows(0), head_major,
                  rows(COL_MR * LANES // D_MODEL), rows(COL_MD * LANES // D_MODEL),
                  weight, weight, weight],
        out_specs=rows(0),
        out_shape=jax.ShapeDtypeStruct((m, D_MODEL), F32),
        compiler_params=pltpu.CompilerParams(dimension_semantics=("parallel",)),
        name="outproj",
    )(x2, o_r, o_d, z2, z2, w_r, w_d, w_o)


def kernel(x, norm_g, w_in, ret_norm_g, ret_w_o, diff_q_norm_g, diff_k_norm_g,
           diff_lq1, diff_lk1, diff_lq2, diff_lk2, diff_sub_norm_g, diff_w_o, w_out):
    b, seq, d = x.shape
    depth = norm_g.shape[0]
    assert d == D_MODEL and w_in.shape[-1] == N_IN
    assert seq % ATT_TQ == 0 and seq % RET_ROWS == 0 and RET_ROWS % RET_CHUNK == 0 and seq % IN_TM == 0

    assert seq <= ATT_TK * 256
    slopes = jnp.exp2(-8.0 * jnp.arange(1, HEADS + 1, dtype=F32) / HEADS)
    log_gamma = jnp.log1p(-jnp.exp2(-5.0 - jnp.arange(HEADS, dtype=F32)))
    row = lambda v: v.reshape(1, -1).astype(F32)

    w_in_b, w_r_b, w_d_b, w_o_b = (w.astype(BF16) for w in (w_in, ret_w_o, diff_w_o, w_out))

    x2 = x.reshape(b * seq, d)
    for l in range(depth):
        lam_init = 0.8 - 0.6 * math.exp(-0.3 * l)
        z2, t4, hd = _inproj(x2, row(norm_g[l]), w_in_b, l, b, seq)
        z3 = z2.reshape(b, seq, N_MAIN)
        o_r = _retention(z3, log_gamma, row(ret_norm_g[l]))
        gq_col = jnp.concatenate([diff_q_norm_g[l], diff_q_norm_g[l]]).reshape(-1, 1).astype(F32)
        gk_row = jnp.concatenate([diff_k_norm_g[l], diff_k_norm_g[l]]).reshape(1, -1).astype(F32)
        o_d = _diffattn(hd, t4, slopes, gq_col, gk_row,
                        row(diff_lq1[l]), row(diff_lk1[l]), row(diff_lq2[l]), row(diff_lk2[l]),
                        row(diff_sub_norm_g[l]), lam_init)
        x2 = _outproj(x2, o_r.reshape(b * seq, -1), o_d, z2, w_r_b, w_d_b, w_o_b, l)
    return x2.reshape(b, seq, d)
```
